```python
import math
import jax
import jax.numpy as jnp
from jax import lax
import numpy as np


D_MODEL = 2048
BATCH = 2
SEQ = 8192
DEPTH = 4

GRID_W = 64
CTX_LEN = 256
D_MIX = D_MODEL
D_SSD = D_MIX // 2
SSD_HEAD_DIM = 64
SSD_HEADS = D_SSD // SSD_HEAD_DIM
SSD_GROUPS = 2
SSD_STATE = 128
SSD_XBC = D_SSD + 2 * SSD_GROUPS * SSD_STATE
D_RET = D_MIX // 4
RET_HEAD_DIM = 128
RET_HEADS = D_RET // RET_HEAD_DIM
D_GDN = D_MIX - D_SSD - D_RET
GDN_HEAD_DIM = 128
GDN_HEADS = D_GDN // GDN_HEAD_DIM
CONV_W = 5
CHUNK = 64
ROPE_BASE = 10000.0
SSD_COLS = D_SSD + SSD_XBC + 2 * SSD_HEADS
RET_COLS = 4 * D_RET
GDN_COLS = 4 * D_GDN + 4 * GDN_HEADS
N_IN_PROJ = SSD_COLS + RET_COLS + GDN_COLS
D_FF = 5632
N_EXPERTS = 8
TOP_K = 2
N_DENSE = (DEPTH + 1) // 2
N_MOE = DEPTH // 2
DEEPNORM_ALPHA = (2 * DEPTH) ** 0.25
DEEPNORM_BETA = (8 * DEPTH) ** -0.25
EPS = 1e-6

kernel_name = 'hybrid_ssd_retnet_gdn_moe_dit'


def layer_norm(x):
    xf = x.astype(jnp.float32)
    mu = jnp.mean(xf, axis=-1, keepdims=True)
    var = jnp.mean(jnp.square(xf - mu), axis=-1, keepdims=True)
    return ((xf - mu) * lax.rsqrt(var + EPS)).astype(x.dtype)


def rms_norm(x):
    xf = x.astype(jnp.float32)
    return (xf * lax.rsqrt(jnp.mean(jnp.square(xf), axis=-1, keepdims=True) + EPS)).astype(x.dtype)


def l2_norm(x):
    xf = x.astype(jnp.float32)
    return (xf * lax.rsqrt(jnp.sum(jnp.square(xf), axis=-1, keepdims=True) + EPS)).astype(x.dtype)


def modulate(x, shift, scale):
    return layer_norm(x) * (1.0 + scale) + shift


def post_norm(x, sub, g, b):
    return layer_norm(DEEPNORM_ALPHA * x + sub) * g + b


def depthwise_conv(x, w):
    return lax.conv_general_dilated(
        x, w[:, None, :].astype(x.dtype), window_strides=(1,),
        padding=[(CONV_W // 2, CONV_W // 2)],
        dimension_numbers=('NWC', 'WIO', 'NWC'), feature_group_count=x.shape[-1])


def rope_tables(rows, head_dim):
    n_freq = head_dim // 4
    inv_freq = ROPE_BASE ** (-jnp.arange(n_freq, dtype=jnp.float32) / n_freq)
    row = jnp.repeat(jnp.arange(rows, dtype=jnp.float32), GRID_W)
    col = jnp.tile(jnp.arange(GRID_W, dtype=jnp.float32), rows)
    ang = jnp.concatenate([row[:, None] * inv_freq, col[:, None] * inv_freq], axis=-1)
    return jnp.cos(ang)[None, :, None, :], jnp.sin(ang)[None, :, None, :]


def apply_rope(x, cos, sin):
    x1, x2 = jnp.split(x, 2, axis=-1)
    return jnp.concatenate([x1 * cos - x2 * sin, x2 * cos + x1 * sin], axis=-1)


def to_chunks(a):
    b_, L = a.shape[0], a.shape[1]
    a = a.reshape((b_, L // CHUNK, CHUNK) + a.shape[2:])
    return jnp.moveaxis(jnp.moveaxis(a, 1, 0), 2, 3)


def from_chunks(y):
    nc, b_, h_, l_, p_ = y.shape
    return jnp.moveaxis(jnp.moveaxis(y, 3, 2), 0, 1).reshape(b_, nc * l_, h_, p_)


def chunk_masks():
    tri = jnp.tril(jnp.ones((CHUNK, CHUNK), dtype=bool))
    return tri, jnp.tril(tri, -1)


def chunked_linear_scan(s0, q, k, v, log_a):
    b_, _, h_, n_ = q.shape
    p_ = v.shape[-1]
    q, k, v = (to_chunks(t.astype(jnp.float32)) for t in (q, k, v))
    g = jnp.cumsum(to_chunks(log_a.astype(jnp.float32)), axis=-1)
    tri, _ = chunk_masks()
    decay = jnp.exp(jnp.where(tri, g[..., :, None] - g[..., None, :], -jnp.inf))
    y_intra = jnp.einsum('cbhls,cbhsp->cbhlp', jnp.einsum('cbhln,cbhsn->cbhls', q, k) * decay, v)
    g_last = g[..., -1]
    chunk_states = jnp.einsum('cbhln,cbhlp->cbhnp', k * jnp.exp(g_last[..., None] - g)[..., None], v)
    if s0 is None:
        s0 = jnp.zeros((b_, h_, n_, p_), jnp.float32)

    def step(s, inp):
        st, d = inp
        return s * d[..., None, None] + st, s

    s_fin, s_in = lax.scan(step, s0, (chunk_states, jnp.exp(g_last)))
    y = y_intra + jnp.einsum('cbhln,cbhnp->cbhlp', q * jnp.exp(g)[..., None], s_in)
    return from_chunks(y), s_fin


def chunked_delta_scan(s0, q, k, v, beta, log_a):
    b_, _, h_, n_ = q.shape
    p_ = v.shape[-1]
    q, k, v = (to_chunks(t.astype(jnp.float32)) for t in (q, k, v))
    beta = to_chunks(beta.astype(jnp.float32))
    g = jnp.cumsum(to_chunks(log_a.astype(jnp.float32)), axis=-1)
    tri, strict = chunk_masks()
    incl = jnp.exp(jnp.where(tri, g[..., :, None] - g[..., None, :], -jnp.inf))
    kb = k * beta[..., None]
    t_mat = jnp.eye(CHUNK, dtype=jnp.float32) + jnp.where(strict, jnp.einsum('cbhln,cbhsn->cbhls', kb, k) * incl, 0.0)
    rhs = jnp.concatenate([v * beta[..., None], kb * jnp.exp(g)[..., None]], axis=-1)
    sol = lax.linalg.triangular_solve(t_mat, rhs, left_side=True, lower=True, unit_diagonal=True)
    u, w = sol[..., :p_], sol[..., p_:]
    attn = jnp.einsum('cbhln,cbhsn->cbhls', q, k) * incl
    g_last = g[..., -1]
    k_dec = k * jnp.exp(g_last[..., None] - g)[..., None]
    q_dec = q * jnp.exp(g)[..., None]
    if s0 is None:
        s0 = jnp.zeros((b_, h_, n_, p_), jnp.float32)

    def step(s, inp):
        q_c, k_c, u_c, w_c, a_c, d_c = inp
        v_new = u_c - jnp.einsum('bhln,bhnp->bhlp', w_c, s)
        o = jnp.einsum('bhln,bhnp->bhlp', q_c, s) + jnp.einsum('bhls,bhsp->bhlp', a_c, v_new)
        s = s * d_c[..., None, None] + jnp.einsum('bhln,bhlp->bhnp', k_c, v_new)
        return s, o

    s_fin, o = lax.scan(step, s0, (q_dec, k_dec, u, w, attn, jnp.exp(g_last)))
    return from_chunks(o), s_fin


def bidir_two_stream(scan, dirs_c, dirs_l):
    flip = lambda t: tuple(jnp.flip(a, axis=1) for a in t)
    y_cf, s_cf = scan(None, *dirs_c[0])
    y_cb, s_cb = scan(None, *flip(dirs_c[1]))
    y_lf, _ = scan(s_cf, *dirs_l[0])
    y_lb, _ = scan(s_cb, *flip(dirs_l[1]))
    return y_cf + jnp.flip(y_cb, axis=1), y_lf + jnp.flip(y_lb, axis=1)


def ssd_mixer(p_c, p_l, conv_w, conv_b, a_log, dt_bias, d_skip, norm_w):
    def prep(p):
        b_, L = p.shape[0], p.shape[1]
        z, xbc, dt = jnp.split(p, [D_SSD, D_SSD + SSD_XBC], axis=-1)
        xbc = jax.nn.silu(depthwise_conv(xbc, conv_w) + conv_b)
        xs, bm, cm = jnp.split(xbc, [D_SSD, D_SSD + SSD_GROUPS * SSD_STATE], axis=-1)
        xs = xs.reshape(b_, L, SSD_HEADS, SSD_HEAD_DIM)
        rep = SSD_HEADS // SSD_GROUPS
        bm = jnp.repeat(bm.reshape(b_, L, SSD_GROUPS, SSD_STATE), rep, axis=2)
        cm = jnp.repeat(cm.reshape(b_, L, SSD_GROUPS, SSD_STATE), rep, axis=2)
        dt = jax.nn.softplus(dt.reshape(b_, L, 2, SSD_HEADS).astype(jnp.float32) + dt_bias)
        log_a = -dt * jnp.exp(a_log.astype(jnp.float32))
        dirs = [(cm, bm, xs * dt[:, :, d, :, None], log_a[:, :, d]) for d in range(2)]
        return z, xs, dirs

    def finish(y, z, xs):
        b_, L = z.shape[0], z.shape[1]
        y = (y + d_skip[:, None] * xs).reshape(b_, L, D_SSD)
        return rms_norm(y * jax.nn.silu(z)) * norm_w

    z_c, xs_c, dirs_c = prep(p_c)
    z_l, xs_l, dirs_l = prep(p_l)
    y_c, y_l = bidir_two_stream(chunked_linear_scan, dirs_c, dirs_l)
    return finish(y_c, z_c, xs_c), finish(y_l, z_l, xs_l)


def retention_mixer(p_c, p_l, log_decay, norm_w, cos, sin):
    def prep(p, rotary):
        b_, L = p.shape[0], p.shape[1]
        q, k, v, g = jnp.split(p, 4, axis=-1)
        q, k, v = (t.reshape(b_, L, RET_HEADS, RET_HEAD_DIM) for t in (q, k, v))
        if rotary:
            q, k = apply_rope(q, cos, sin), apply_rope(k, cos, sin)
        k = k * RET_HEAD_DIM ** -0.5
        dirs = [(q, k, v, jnp.broadcast_to(log_decay[d], (b_, L, RET_HEADS))) for d in range(2)]
        return g, dirs

    def finish(y, g):
        b_, L = g.shape[0], g.shape[1]
        y = layer_norm(y).reshape(b_, L, D_RET) * norm_w
        return jax.nn.silu(g) * y

    g_c, dirs_c = prep(p_c, False)
    g_l, dirs_l = prep(p_l, True)
    y_c, y_l = bidir_two_stream(chunked_linear_scan, dirs_c, dirs_l)
    return finish(y_c, g_c), finish(y_l, g_l)


def gdn_mixer(p_c, p_l, conv_w, a_log, dt_bias, norm_w):
    def prep(p):
        b_, L = p.shape[0], p.shape[1]
        qkv, g, a_raw, b_raw = jnp.split(p, [3 * D_GDN, 4 * D_GDN, 4 * D_GDN + 2 * GDN_HEADS], axis=-1)
        qkv = jax.nn.silu(depthwise_conv(qkv, conv_w))
        q, k, v = (t.reshape(b_, L, GDN_HEADS, GDN_HEAD_DIM) for t in jnp.split(qkv, 3, axis=-1))
        q = l2_norm(q) * GDN_HEAD_DIM ** -0.5
        k = l2_norm(k)
        a_raw = a_raw.reshape(b_, L, 2, GDN_HEADS).astype(jnp.float32)
        log_a = -jnp.exp(a_log.astype(jnp.float32)) * jax.nn.softplus(a_raw + dt_bias)
        beta = jax.nn.sigmoid(b_raw.reshape(b_, L, 2, GDN_HEADS).astype(jnp.float32))
        dirs = [(q, k, v, beta[:, :, d], log_a[:, :, d]) for d in range(2)]
        return g, dirs

    def finish(y, g):
        b_, L = g.shape[0], g.shape[1]
        y = (rms_norm(y) * norm_w).reshape(b_, L, D_GDN)
        return jax.nn.silu(g) * y

    g_c, dirs_c = prep(p_c)
    g_l, dirs_l = prep(p_l)
    y_c, y_l = bidir_two_stream(chunked_delta_scan, dirs_c, dirs_l)
    return finish(y_c, g_c), finish(y_l, g_l)


def swiglu(h, w1, w3, w2):
    return (jax.nn.silu(h @ w1) * (h @ w3)) @ w2


def moe_swiglu(h, router, w1, w3, w2):
    logits = (h @ router).astype(jnp.float32)
    top_v, top_i = lax.top_k(logits, TOP_K)
    gates = jax.nn.softmax(top_v, axis=-1)
    dense_gate = jnp.sum(jax.nn.one_hot(top_i, N_EXPERTS, dtype=jnp.float32) * gates[..., None], axis=-2)
    y = jnp.zeros(h.shape, jnp.float32)
    for e in range(N_EXPERTS):
        y = y + dense_gate[..., e:e + 1] * swiglu(h, w1[e], w3[e], w2[e])
    return y.astype(h.dtype)


def channel_mixer(h, i, ffn_w1, ffn_w3, ffn_w2, moe_router, moe_w1, moe_w3, moe_w2):
    j = i // 2
    if i % 2 == 0:
        return swiglu(h, ffn_w1[j], ffn_w3[j], ffn_w2[j])
    return moe_swiglu(h, moe_router[j], moe_w1[j], moe_w3[j], moe_w2[j])


def setup_inputs(seed: int = 0) -> dict:
    key = jax.random.key(seed)
    ks = jax.random.split(key, 32)
    f32 = jnp.float32

    def nrm(i, shape, scale):
        return jax.random.normal(ks[i], shape, f32) * scale

    def gain(i, shape):
        return 1.0 + nrm(i, shape, 0.1)

    def a_log_init(i, shape):
        return jnp.log(jax.random.uniform(ks[i], shape, f32, 1.0, 16.0))

    def dt_bias_init(i, shape):
        dt = jnp.exp(jax.random.uniform(ks[i], shape, f32, math.log(1e-3), math.log(1e-1)))
        return dt + jnp.log(-jnp.expm1(-dt))

    u = jax.random.uniform(ks[13], (DEPTH, 2, RET_HEADS), f32)
    ret_log_decay = jnp.log1p(-jnp.exp2(-5.0 - jnp.arange(RET_HEADS, dtype=f32) - u))
    return {
        'x': nrm(0, (BATCH, SEQ, D_MODEL), 1.0),
        'c': nrm(1, (BATCH, D_MODEL), 1.0),
        'ctx': nrm(2, (BATCH, CTX_LEN, D_MODEL), 1.0),
        'c_ctx': nrm(3, (D_MODEL,), 1.0),
        'w_mod': nrm(4, (DEPTH, D_MODEL, 6 * D_MODEL), 0.5 * D_MODEL ** -0.5),
        'b_mod': nrm(5, (DEPTH, 6 * D_MODEL), 0.02),
        'w_in': nrm(6, (DEPTH, D_MODEL, N_IN_PROJ), D_MODEL ** -0.5),
        'ssd_conv_w': nrm(7, (DEPTH, CONV_W, SSD_XBC), CONV_W ** -0.5),
        'ssd_conv_b': nrm(8, (DEPTH, SSD_XBC), 0.02),
        'ssd_a_log': a_log_init(9, (DEPTH, 2, SSD_HEADS)),
        'ssd_dt_bias': dt_bias_init(10, (DEPTH, 2, SSD_HEADS)),
        'ssd_d': gain(11, (DEPTH, SSD_HEADS)),
        'ssd_norm_w': gain(12, (DEPTH, D_SSD)),
        'ret_log_decay': ret_log_decay,
        'ret_norm_w': gain(14, (DEPTH, D_RET)),
        'gdn_conv_w': nrm(15, (DEPTH, CONV_W, 3 * D_GDN), CONV_W ** -0.5),
        'gdn_a_log': a_log_init(16, (DEPTH, 2, GDN_HEADS)),
        'gdn_dt_bias': dt_bias_init(17, (DEPTH, 2, GDN_HEADS)),
        'gdn_norm_w': gain(18, (DEPTH, GDN_HEAD_DIM)),
        'w_out': nrm(19, (DEPTH, D_MIX, D_MODEL), DEEPNORM_BETA * D_MIX ** -0.5),
        'ln1_g': gain(20, (DEPTH, D_MODEL)),
        'ln1_b': nrm(21, (DEPTH, D_MODEL), 0.02),
        'ln2_g': gain(22, (DEPTH, D_MODEL)),
        'ln2_b': nrm(23, (DEPTH, D_MODEL), 0.02),
        'ffn_w1': nrm(24, (N_DENSE, D_MODEL, D_FF), D_MODEL ** -0.5),
        'ffn_w3': nrm(25, (N_DENSE, D_MODEL, D_FF), D_MODEL ** -0.5),
        'ffn_w2': nrm(26, (N_DENSE, D_FF, D_MODEL), DEEPNORM_BETA * D_FF ** -0.5),
        'moe_router': nrm(27, (N_MOE, D_MODEL, N_EXPERTS), D_MODEL ** -0.5),
        'moe_w1': nrm(28, (N_MOE, N_EXPERTS, D_MODEL, D_FF), D_MODEL ** -0.5),
        'moe_w3': nrm(29, (N_MOE, N_EXPERTS, D_MODEL, D_FF), D_MODEL ** -0.5),
        'moe_w2': nrm(30, (N_MOE, N_EXPERTS, D_FF, D_MODEL), DEEPNORM_BETA * D_FF ** -0.5),
    }


def reference(x, c, ctx, c_ctx, w_mod, b_mod, w_in, ssd_conv_w, ssd_conv_b, ssd_a_log, ssd_dt_bias,
              ssd_d, ssd_norm_w, ret_log_decay, ret_norm_w, gdn_conv_w, gdn_a_log, gdn_dt_bias, gdn_norm_w,
              w_out, ln1_g, ln1_b, ln2_g, ln2_b, ffn_w1, ffn_w3, ffn_w2, moe_router, moe_w1, moe_w3, moe_w2):
    rows = x.shape[1] // GRID_W
    cos, sin = rope_tables(rows, RET_HEAD_DIM)
    c_act = jax.nn.silu(c)
    cc_act = jax.nn.silu(c_ctx)
    x_l, x_c = x, ctx
    for i in range(DEPTH):
        last = i == DEPTH - 1
        mod_l = jnp.split((c_act @ w_mod[i] + b_mod[i])[:, None, :], 6, axis=-1)
        mod_c = jnp.split(cc_act @ w_mod[i] + b_mod[i], 6, axis=-1)
        p_c = modulate(x_c, mod_c[0], mod_c[1]) @ w_in[i]
        p_l = modulate(x_l, mod_l[0], mod_l[1]) @ w_in[i]
        ssd_c, ret_c, gdn_c = jnp.split(p_c, [SSD_COLS, SSD_COLS + RET_COLS], axis=-1)
        ssd_l, ret_l, gdn_l = jnp.split(p_l, [SSD_COLS, SSD_COLS + RET_COLS], axis=-1)
        ys_c, ys_l = ssd_mixer(ssd_c, ssd_l, ssd_conv_w[i], ssd_conv_b[i], ssd_a_log[i], ssd_dt_bias[i],
                               ssd_d[i], ssd_norm_w[i])
        yr_c, yr_l = retention_mixer(ret_c, ret_l, ret_log_decay[i], ret_norm_w[i], cos, sin)
        yg_c, yg_l = gdn_mixer(gdn_c, gdn_l, gdn_conv_w[i], gdn_a_log[i], gdn_dt_bias[i], gdn_norm_w[i])
        o_l = jnp.concatenate([ys_l, yr_l, yg_l], axis=-1) @ w_out[i]
        x_l = post_norm(x_l, mod_l[2] * o_l, ln1_g[i], ln1_b[i])
        f_l = channel_mixer(modulate(x_l, mod_l[3], mod_l[4]), i, ffn_w1, ffn_w3, ffn_w2,
                            moe_router, moe_w1, moe_w3, moe_w2)
        x_l = post_norm(x_l, mod_l[5] * f_l, ln2_g[i], ln2_b[i])
        if not last:
            o_c = jnp.concatenate([ys_c, yr_c, yg_c], axis=-1) @ w_out[i]
            x_c = post_norm(x_c, mod_c[2] * o_c, ln1_g[i], ln1_b[i])
            f_c = channel_mixer(modulate(x_c, mod_c[3], mod_c[4]), i, ffn_w1, ffn_w3, ffn_w2,
                                moe_router, moe_w1, moe_w3, moe_w2)
            x_c = post_norm(x_c, mod_c[5] * f_c, ln2_g[i], ln2_b[i])
    return x_l
```

```python
import functools

import jax
import jax.numpy as jnp
from jax import lax
from jax.experimental import pallas as pl
from jax.experimental.pallas import tpu as pltpu

F32 = jnp.float32
BF16 = jnp.bfloat16

D_MODEL = 2048
DEPTH = 4
GRID_W = 64
D_SSD = 1024
SSD_HEAD_DIM = 64
SSD_HEADS = 16
SSD_GROUPS = 2
SSD_STATE = 128
SSD_XBC = D_SSD + 2 * SSD_GROUPS * SSD_STATE
D_RET = 512
RET_HEAD_DIM = 128
RET_HEADS = 4
D_GDN = 512
GDN_HEAD_DIM = 128
GDN_HEADS = 4
CONV_W = 5
ROPE_BASE = 10000.0
D_FF = 5632
N_EXPERTS = 8
DEEPNORM_ALPHA = (2 * DEPTH) ** 0.25
EPS = 1e-6

LANES = 128
SUBLANES = 8
VMEM_LIMIT = 56 * 1024 * 1024

TM = 512
TM_OUT = 256
PREP_ROWS = 256
CHUNK = 128
TN_IN = 512
TF = 512
TN_MOD = 1024

COL_CONV = 0
COL_Z = 3072
COL_RET = 4096
COL_GG = 6144
N_MAIN = 6656
SM_DT = 0
SM_A = 32
SM_B = 40


def _silu(x):
    return x * jax.nn.sigmoid(x)


def _softplus(x):
    return jnp.maximum(x, 0.0) + jnp.log(1.0 + jnp.exp(-jnp.abs(x)))


def _ln(x):
    mu = jnp.mean(x, axis=-1, keepdims=True)
    xc = x - mu
    var = jnp.mean(xc * xc, axis=-1, keepdims=True)
    return xc * lax.rsqrt(var + EPS)


def _dot(a, b):
    return jnp.dot(a, b, preferred_element_type=F32)


def _dot_nt(a, b):
    return lax.dot_general(a, b, (((1,), (1,)), ((), ())), preferred_element_type=F32)


def _dot_tn(a, b):
    return lax.dot_general(a, b, (((0,), (0,)), ((), ())), preferred_element_type=F32)


def _split2(x):
    hi = x.astype(BF16)
    lo = (x - hi.astype(F32)).astype(BF16)
    return hi, lo


def _split3(x):
    hi = x.astype(BF16)
    r = x - hi.astype(F32)
    mid = r.astype(BF16)
    lo = (r - mid.astype(F32)).astype(BF16)
    return hi, mid, lo


def _dot_hi(a, b):
    ah, al = _split2(a)
    bh, bl = _split2(b)
    return _dot(ah, bh) + _dot(ah, bl) + _dot(al, bh)


def _dot01_lhs(m01, a):
    hi, mid, lo = _split3(a)
    return _dot(m01, hi) + _dot(m01, mid) + _dot(m01, lo)


def _dot01_rhs(a, m01):
    hi, mid, lo = _split3(a)
    return _dot(hi, m01) + _dot(mid, m01) + _dot(lo, m01)


def _cparams(sem):
    return pltpu.CompilerParams(dimension_semantics=sem, vmem_limit_bytes=VMEM_LIMIT)


def _mods_kernel(cb_ref, w_ref, b_ref, o_ref):
    tn = w_ref.shape[2]
    rows = []
    for r in range(3):
        act = _silu(cb_ref[r])
        pieces = [jnp.sum(w_ref[0, :, j * LANES:(j + 1) * LANES] * act, axis=0, keepdims=True)
                  for j in range(tn // LANES)]
        rows.append(jnp.concatenate(pieces, axis=1) + b_ref[0])
    rows.append(jnp.zeros((SUBLANES - 3, tn), F32))
    o_ref[0] = jnp.concatenate(rows, axis=0)


def _mods(c, c_ctx, w_mod, b_mod):
    cvec = jnp.concatenate([c, c_ctx[None, :]], axis=0)
    cb = jnp.broadcast_to(cvec[:, :, None], (3, D_MODEL, LANES))
    n = w_mod.shape[2]
    return pl.pallas_call(
        _mods_kernel,
        out_shape=jax.ShapeDtypeStruct((DEPTH, SUBLANES, n), F32),
        grid=(DEPTH, n // TN_MOD),
        in_specs=[pl.BlockSpec((3, D_MODEL, LANES), lambda l, j: (0, 0, 0)),
                  pl.BlockSpec((1, D_MODEL, TN_MOD), lambda l, j: (l, 0, j)),
                  pl.BlockSpec((1, 1, TN_MOD), lambda l, j: (l, 0, j))],
        out_specs=pl.BlockSpec((1, SUBLANES, TN_MOD), lambda l, j: (l, 0, j)),
        compiler_params=_cparams(("arbitrary", "arbitrary")),
        name="mods",
    )(cb, w_mod, b_mod.reshape(DEPTH, 1, n))


def _mod_row(ref, tiles_per_batch):
    s = jnp.minimum(pl.program_id(0) // tiles_per_batch, 2)
    return ref[pl.ds(s, 1), :]


def _inproj_kernel(x_ref, sh_ref, sc_ref, w_ref, ws_ref, wst_ref, p_ref, ps_ref, pst_ref, h_scr,
                   *, tiles_per_batch):
    @pl.when(pl.program_id(1) == 0)
    def _():
        shift = _mod_row(sh_ref, tiles_per_batch)
        scale = _mod_row(sc_ref, tiles_per_batch)
        hb = (_ln(x_ref[...]) * (1.0 + scale) + shift).astype(BF16)
        h_scr[...] = hb
        ps_ref[...] = _dot(hb, ws_ref[...])
        pst_ref[...] = _dot_nt(wst_ref[...], hb)

    p_ref[...] = _dot(h_scr[...], w_ref[...])


def _inproj(x, mods_l, w_main, w_small, w_small_t, tiles_per_batch):
    nt = x.shape[0]
    return pl.pallas_call(
        functools.partial(_inproj_kernel, tiles_per_batch=tiles_per_batch),
        out_shape=(jax.ShapeDtypeStruct((nt, N_MAIN), F32),
                   jax.ShapeDtypeStruct((nt, LANES), F32),
                   jax.ShapeDtypeStruct((LANES, nt), F32)),
        grid=(nt // TM, N_MAIN // TN_IN),
        in_specs=[pl.BlockSpec((TM, D_MODEL), lambda i, j: (i, 0)),
                  pl.BlockSpec((SUBLANES, D_MODEL), lambda i, j: (0, 0)),
                  pl.BlockSpec((SUBLANES, D_MODEL), lambda i, j: (0, 1)),
                  pl.BlockSpec((D_MODEL, TN_IN), lambda i, j: (0, j)),
                  pl.BlockSpec((D_MODEL, LANES), lambda i, j: (0, 0)),
                  pl.BlockSpec((LANES, D_MODEL), lambda i, j: (0, 0))],
        out_specs=(pl.BlockSpec((TM, TN_IN), lambda i, j: (i, j)),
                   pl.BlockSpec((TM, LANES), lambda i, j: (i, 0)),
                   pl.BlockSpec((LANES, TM), lambda i, j: (0, i))),
        scratch_shapes=[pltpu.VMEM((TM, D_MODEL), BF16)],
        compiler_params=_cparams(("arbitrary", "arbitrary")),
        name="inproj",
    )(x, mods_l, mods_l, w_main, w_small, w_small_t)


def _prep_kernel(main_ref, prev_ref, next_ref, rq_ref, rk_ref, cos_ref, sin_ref, cw_ref, cb_ref,
                 o_ref, ext_scr, *, seg_starts, seg_ends):
    i = pl.program_id(0)
    r = main_ref.shape[0]
    is_start = functools.reduce(jnp.logical_or, [i == s for s in seg_starts])
    is_end = functools.reduce(jnp.logical_or, [i == s for s in seg_ends])
    ext_scr[0:SUBLANES, :] = jnp.where(is_start, 0.0, prev_ref[...])
    ext_scr[SUBLANES:SUBLANES + r, :] = main_ref[...]
    ext_scr[SUBLANES + r:2 * SUBLANES + r, :] = jnp.where(is_end, 0.0, next_ref[...])

    half = CONV_W // 2
    n_conv = main_ref.shape[1]
    for c0 in range(0, n_conv, 512):
        cs = slice(c0, c0 + 512)
        acc = jnp.broadcast_to(cb_ref[:, cs], (r, 512))
        for j in range(CONV_W):
            acc = acc + ext_scr[SUBLANES - half + j:SUBLANES - half + j + r, cs] * cw_ref[j:j + 1, cs]
        act = _silu(acc)
        if c0 in (1536, 2048):
            scale = GDN_HEAD_DIM ** -0.5 if c0 == 1536 else 1.0
            for h in range(GDN_HEADS):
                xh = act[:, h * LANES:(h + 1) * LANES]
                inv = lax.rsqrt(jnp.sum(xh * xh, axis=-1, keepdims=True) + EPS)
                o_ref[:, c0 + h * LANES:c0 + (h + 1) * LANES] = xh * inv * scale
        else:
            o_ref[:, cs] = act

    cosf = cos_ref[...]
    sinf = sin_ref[...]
    for src, off, scale in ((rq_ref, 3072, 1.0), (rk_ref, 3584, RET_HEAD_DIM ** -0.5)):
        for h in range(RET_HEADS):
            xh = src[:, h * LANES:(h + 1) * LANES]
            rot = xh * cosf + pltpu.roll(xh, LANES // 2, axis=1) * sinf
            o_ref[:, off + h * LANES:off + (h + 1) * LANES] = rot * scale


def _prep(p_main, cos_t, sin_t, conv_w, conv_b, b, l, lc):
    nt = p_main.shape[0]
    r = PREP_ROWS
    n_chunks = nt // r
    lat_chunks = (b * l) // r
    seg_rows = [k * l for k in range(b)] + [b * l + k * lc for k in range(b)]
    seg_len = [l] * b + [lc] * b
    seg_starts = tuple(s // r for s in seg_rows)
    seg_ends = tuple((s + n) // r - 1 for s, n in zip(seg_rows, seg_len))
    sub_per = r // SUBLANES
    n_sub = nt // SUBLANES
    rope_map = lambda i: (jnp.where(i < lat_chunks, i % (l // r), l // r), 0)
    return pl.pallas_call(
        functools.partial(_prep_kernel, seg_starts=seg_starts, seg_ends=seg_ends),
        out_shape=jax.ShapeDtypeStruct((nt, 4096), F32),
        grid=(n_chunks,),
        in_specs=[pl.BlockSpec((r, 3072), lambda i: (i, 0)),
                  pl.BlockSpec((SUBLANES, 3072), lambda i: (jnp.maximum(i * sub_per - 1, 0), 0)),
                  pl.BlockSpec((SUBLANES, 3072), lambda i: (jnp.minimum((i + 1) * sub_per, n_sub - 1), 0)),
                  pl.BlockSpec((r, 512), lambda i: (i, COL_RET // 512)),
                  pl.BlockSpec((r, 512), lambda i: (i, COL_RET // 512 + 1)),
                  pl.BlockSpec((r, LANES), rope_map),
                  pl.BlockSpec((r, LANES), rope_map),
                  pl.BlockSpec((CONV_W, 3072), lambda i: (0, 0)),
                  pl.BlockSpec((1, 3072), lambda i: (0, 0))],
        out_specs=pl.BlockSpec((r, 4096), lambda i: (i, 0)),
        scratch_shapes=[pltpu.VMEM((r + 2 * SUBLANES, 3072), F32)],
        compiler_params=_cparams(("arbitrary",)),
        name="prep",
    )(p_main, p_main, p_main, p_main, p_main, cos_t, sin_t, conv_w, conv_b)


def _chunk_maps(b, l, lc):
    ncc, ncl = lc // CHUNK, l // CHUNK
    ctx0 = (b * l) // CHUNK

    def fwd(bi, s):
        return jnp.where(s < ncc, ctx0 + bi * ncc + s, bi * ncl + s - ncc)

    def bwd(bi, s):
        return jnp.where(s < ncc, ctx0 + bi * ncc + (ncc - 1 - s), bi * ncl + (ncl - 1 - (s - ncc)))

    return (fwd, bwd), ncc + ncl


def _tri_masks():
    tt = lax.broadcasted_iota(jnp.int32, (CHUNK, CHUNK), 0)
    ss = lax.broadcasted_iota(jnp.int32, (CHUNK, CHUNK), 1)
    return ss <= tt, ss >= tt


def _cumsums(la_c, la_r, d, low, upp):
    lowf = jnp.where(low, 1.0, 0.0).astype(BF16)
    uppf = jnp.where(upp, 1.0, 0.0).astype(BF16)
    if d == 0:
        return _dot01_lhs(lowf, la_c), _dot01_rhs(la_r, uppf), low
    return _dot01_lhs(uppf, la_c), _dot01_rhs(la_r, lowf), upp


def _ssd_kernel(xs_f, bc_f, sm_f, smt_f, xs_b, bc_b, sm_b, smt_b, prow_ref, pcol_ref,
                y_f, y_b, s_scr):
    @pl.when(pl.program_id(1) == 0)
    def _():
        s_scr[...] = jnp.zeros_like(s_scr)

    low, upp = _tri_masks()
    bias_r, alog_r = prow_ref[0:1, :], prow_ref[1:2, :]
    bias_c, alog_c = pcol_ref[:, 0:1], pcol_ref[:, 1:2]
    for d, (xs_ref, bc_ref, sm_ref, smt_ref, y_ref) in enumerate(
            ((xs_f, bc_f, sm_f, smt_f, y_f), (xs_b, bc_b, sm_b, smt_b, y_b))):
        dt_c = _softplus(sm_ref[...] + bias_r)
        dt_r = _softplus(smt_ref[...] + bias_c)
        g_c, g_r, mask = _cumsums(-dt_c * jnp.exp(alog_r), -dt_r * jnp.exp(alog_c), d, low, upp)
        last = CHUNK - 1 if d == 0 else 0
        heads_per_group = SSD_HEADS // SSD_GROUPS
        for g in range(SSD_GROUPS):
            kg = bc_ref[:, g * SSD_STATE:(g + 1) * SSD_STATE].astype(BF16)
            qg = bc_ref[:, (SSD_GROUPS + g) * SSD_STATE:(SSD_GROUPS + g + 1) * SSD_STATE].astype(BF16)
            qk = _dot_nt(qg, kg)
            for hh in range(heads_per_group):
                h = g * heads_per_group + hh
                ln = SM_DT + d * SSD_HEADS + h
                hs = slice(h * SSD_HEAD_DIM, (h + 1) * SSD_HEAD_DIM)
                gcol = g_c[:, ln:ln + 1]
                grow = g_r[ln:ln + 1, :]
                glast = g_c[last:last + 1, ln:ln + 1]
                decay = jnp.exp(jnp.where(mask, gcol - grow, -jnp.inf))
                v = xs_ref[:, hs] * dt_c[:, ln:ln + 1]
                state = s_scr[d, :, hs]
                y_ref[:, hs] = (_dot((qk * decay).astype(BF16), v.astype(BF16))
                                + jnp.exp(gcol) * _dot(qg, state.astype(BF16)))
                vdec = (v * jnp.exp(glast - gcol)).astype(BF16)
                s_scr[d, :, hs] = state * jnp.exp(glast) + _dot_tn(kg, vdec)


def _ssd_scan(prep, p_small, p_small_t, prow, pcol, b, l, lc):
    nt = prep.shape[0]
    maps, steps = _chunk_maps(b, l, lc)
    in_specs = []
    for m in maps:
        in_specs += [pl.BlockSpec((CHUNK, D_SSD), lambda bi, s, m=m: (m(bi, s), 0)),
                     pl.BlockSpec((CHUNK, 512), lambda bi, s, m=m: (m(bi, s), 2)),
                     pl.BlockSpec((CHUNK, LANES), lambda bi, s, m=m: (m(bi, s), 0)),
                     pl.BlockSpec((LANES, CHUNK), lambda bi, s, m=m: (0, m(bi, s)))]
    in_specs += [pl.BlockSpec((SUBLANES, LANES), lambda bi, s: (0, 0)),
                 pl.BlockSpec((LANES, SUBLANES), lambda bi, s: (0, 0))]
    return pl.pallas_call(
        _ssd_kernel,
        out_shape=(jax.ShapeDtypeStruct((nt, D_SSD), F32),) * 2,
        grid=(b, steps),
        in_specs=in_specs,
        out_specs=tuple(pl.BlockSpec((CHUNK, D_SSD), lambda bi, s, m=m: (m(bi, s), 0)) for m in maps),
        scratch_shapes=[pltpu.VMEM((2, SSD_STATE, D_SSD), F32)],
        compiler_params=_cparams(("arbitrary", "arbitrary")),
        name="ssd_scan",
    )(prep, prep, p_small, p_small_t, prep, prep, p_small, p_small_t, prow, pcol)


def _ret_kernel(q_f, k_f, v_f, q_b, k_b, v_b, ld_ref, y_f, y_b, s_scr):
    @pl.when(pl.program_id(1) == 0)
    def _():
        s_scr[...] = jnp.zeros_like(s_scr)

    low, upp = _tri_masks()
    tt = lax.broadcasted_iota(jnp.int32, (CHUNK, CHUNK), 0)
    ss = lax.broadcasted_iota(jnp.int32, (CHUNK, CHUNK), 1)
    tcol = lax.broadcasted_iota(jnp.int32, (CHUNK, 1), 0)
    for d, (q_ref, k_ref, v_ref, y_ref) in enumerate(((q_f, k_f, v_f, y_f), (q_b, k_b, v_b, y_b))):
        mask = low if d == 0 else upp
        dist = (tt - ss if d == 0 else ss - tt).astype(F32)
        n_in = (tcol + 1 if d == 0 else CHUNK - tcol).astype(F32)
        n_out = (CHUNK - 1 - tcol if d == 0 else tcol).astype(F32)
        for h in range(RET_HEADS):
            hs = slice(h * RET_HEAD_DIM, (h + 1) * RET_HEAD_DIM)
            ld = ld_ref[d * RET_HEADS + h:d * RET_HEADS + h + 1, :]
            ld1 = ld[:, 0:1]
            decay = jnp.exp(jnp.where(mask, dist * ld, -jnp.inf))
            q = q_ref[:, hs].astype(BF16)
            k = k_ref[:, hs]
            v = v_ref[:, hs]
            state = s_scr[d, h]
            y_ref[:, hs] = (_dot((_dot_nt(q, k.astype(BF16)) * decay).astype(BF16), v.astype(BF16))
                            + jnp.exp(n_in * ld1) * _dot(q, state.astype(BF16)))
            kdec = (k * jnp.exp(n_out * ld1)).astype(BF16)
            s_scr[d, h] = state * jnp.exp(CHUNK * ld1) + _dot_tn(kdec, v.astype(BF16))


def _ret_scan(prep, p_main, ld, b, l, lc):
    nt = prep.shape[0]
    maps, steps = _chunk_maps(b, l, lc)
    in_specs = []
    for m in maps:
        in_specs += [pl.BlockSpec((CHUNK, D_RET), lambda bi, s, m=m: (m(bi, s), 6)),
                     pl.BlockSpec((CHUNK, D_RET), lambda bi, s, m=m: (m(bi, s), 7)),
                     pl.BlockSpec((CHUNK, D_RET), lambda bi, s, m=m: (m(bi, s), COL_RET // 512 + 2))]
    in_specs += [pl.BlockSpec((SUBLANES, LANES), lambda bi, s: (0, 0))]
    return pl.pallas_call(
        _ret_kernel,
        out_shape=(jax.ShapeDtypeStruct((nt, D_RET), F32),) * 2,
        grid=(b, steps),
        in_specs=in_specs,
        out_specs=tuple(pl.BlockSpec((CHUNK, D_RET), lambda bi, s, m=m: (m(bi, s), 0)) for m in maps),
        scratch_shapes=[pltpu.VMEM((2, RET_HEADS, RET_HEAD_DIM, RET_HEAD_DIM), F32)],
        compiler_params=_cparams(("arbitrary", "arbitrary")),
        name="ret_scan",
    )(prep, prep, p_main, prep, prep, p_main, ld)


def _gdn_kernel(q_f, k_f, v_f, sm_f, smt_f, q_b, k_b, v_b, sm_b, smt_b, prow_ref, pcol_ref,
                o_f, o_b, s_scr):
    @pl.when(pl.program_id(1) == 0)
    def _():
        s_scr[...] = jnp.zeros_like(s_scr)

    low, upp = _tri_masks()
    bias_r, alog_r = prow_ref[0:1, :], prow_ref[1:2, :]
    bias_c, alog_c = pcol_ref[:, 0:1], pcol_ref[:, 1:2]
    n_doublings = CHUNK.bit_length() - 1
    tt = lax.broadcasted_iota(jnp.int32, (CHUNK, CHUNK), 0)
    ss = lax.broadcasted_iota(jnp.int32, (CHUNK, CHUNK), 1)
    for d, (q_ref, k_ref, v_ref, sm_ref, smt_ref, o_ref) in enumerate(
            ((q_f, k_f, v_f, sm_f, smt_f, o_f), (q_b, k_b, v_b, sm_b, smt_b, o_b))):
        sm = sm_ref[...]
        la_c = -jnp.exp(alog_r) * _softplus(sm + bias_r)
        la_r = -jnp.exp(alog_c) * _softplus(smt_ref[...] + bias_c)
        beta_c = jax.nn.sigmoid(sm)
        g_c, g_r, mask = _cumsums(la_c, la_r, d, low, upp)
        strict = jnp.logical_and(low, jnp.logical_not(upp)) if d == 0 else jnp.logical_and(upp, jnp.logical_not(low))
        last = CHUNK - 1 if d == 0 else 0
        for h in range(GDN_HEADS):
            hs = slice(h * GDN_HEAD_DIM, (h + 1) * GDN_HEAD_DIM)
            la_ln = SM_A + d * GDN_HEADS + h
            b_ln = SM_B + d * GDN_HEADS + h
            gcol = g_c[:, la_ln:la_ln + 1]
            grow = g_r[la_ln:la_ln + 1, :]
            glast = g_c[last:last + 1, la_ln:la_ln + 1]
            beta = beta_c[:, b_ln:b_ln + 1]
            incl = jnp.exp(jnp.where(mask, gcol - grow, -jnp.inf))
            q = q_ref[:, hs]
            k = k_ref[:, hs]
            v = v_ref[:, hs]
            kb = k * beta
            k16 = k.astype(BF16)
            a_mat = jnp.where(strict, _dot_nt(kb.astype(BF16), k16) * incl, 0.0)
            inv = jnp.where(low & upp, 1.0, 0.0)
            for j in range(n_doublings):
                s = 1 << j
                same = (tt >> (j + 1)) == (ss >> (j + 1))
                t_hi, s_hi = (tt & s) != 0, (ss & s) != 0
                join = same & (t_hi & ~s_hi if d == 0 else ~t_hi & s_hi)
                inv = inv - _dot_hi(inv, _dot_hi(jnp.where(join, a_mat, 0.0), inv))
            sol = _dot_hi(inv, jnp.concatenate([v * beta, kb * jnp.exp(gcol)], axis=1))
            u = sol[:, :GDN_HEAD_DIM]
            w = sol[:, GDN_HEAD_DIM:]
            attn = _dot_nt(q.astype(BF16), k16) * incl
            state = s_scr[d, h]
            s16 = state.astype(BF16)
            v_new = u - _dot(w.astype(BF16), s16)
            vn16 = v_new.astype(BF16)
            o_ref[:, hs] = _dot((q * jnp.exp(gcol)).astype(BF16), s16) + _dot(attn.astype(BF16), vn16)
            kdec = (k * jnp.exp(glast - gcol)).astype(BF16)
            s_scr[d, h] = state * jnp.exp(glast) + _dot_tn(kdec, vn16)


def _gdn_scan(prep, p_small, p_small_t, prow, pcol, b, l, lc):
    nt = prep.shape[0]
    maps, steps = _chunk_maps(b, l, lc)
    in_specs = []
    for m in maps:
        in_specs += [pl.BlockSpec((CHUNK, D_GDN), lambda bi, s, m=m: (m(bi, s), 3)),
                     pl.BlockSpec((CHUNK, D_GDN), lambda bi, s, m=m: (m(bi, s), 4)),
                     pl.BlockSpec((CHUNK, D_GDN), lambda bi, s, m=m: (m(bi, s), 5)),
                     pl.BlockSpec((CHUNK, LANES), lambda bi, s, m=m: (m(bi, s), 0)),
                     pl.BlockSpec((LANES, CHUNK), lambda bi, s, m=m: (0, m(bi, s)))]
    in_specs += [pl.BlockSpec((SUBLANES, LANES), lambda bi, s: (0, 0)),
                 pl.BlockSpec((LANES, SUBLANES), lambda bi, s: (0, 0))]
    return pl.pallas_call(
        _gdn_kernel,
        out_shape=(jax.ShapeDtypeStruct((nt, D_GDN), F32),) * 2,
        grid=(b, steps),
        in_specs=in_specs,
        out_specs=tuple(pl.BlockSpec((CHUNK, D_GDN), lambda bi, s, m=m: (m(bi, s), 0)) for m in maps),
        scratch_shapes=[pltpu.VMEM((2, GDN_HEADS, GDN_HEAD_DIM, GDN_HEAD_DIM), F32)],
        compiler_params=_cparams(("arbitrary", "arbitrary")),
        name="gdn_scan",
    )(prep, prep, prep, p_small, p_small_t, prep, prep, prep, p_small, p_small_t, prow, pcol)


def _outproj_kernel(x_ref, gate_ref, ysf, ysb, xs_ref, z_ref, yrf, yrb, rg_ref, ogf, ogb, gg_ref,
                    dskip_ref, snw_ref, rnw_ref, gnw_ref, w_ref, lng_ref, lnb_ref, o_ref, cat_scr,
                    *, tiles_per_batch):
    y = ysf[...] + ysb[...] + dskip_ref[...] * xs_ref[...]
    y = y * _silu(z_ref[...])
    y = y * lax.rsqrt(jnp.mean(y * y, axis=-1, keepdims=True) + EPS)
    cat_scr[:, 0:D_SSD] = (y * snw_ref[...]).astype(BF16)
    for h in range(RET_HEADS):
        hs = slice(h * RET_HEAD_DIM, (h + 1) * RET_HEAD_DIM)
        yh = _ln(yrf[:, hs] + yrb[:, hs]) * rnw_ref[:, hs]
        cat_scr[:, D_SSD + h * RET_HEAD_DIM:D_SSD + (h + 1) * RET_HEAD_DIM] = (_silu(rg_ref[:, hs]) * yh).astype(BF16)
    off = D_SSD + D_RET
    for h in range(GDN_HEADS):
        hs = slice(h * GDN_HEAD_DIM, (h + 1) * GDN_HEAD_DIM)
        yh = ogf[:, hs] + ogb[:, hs]
        yh = yh * lax.rsqrt(jnp.mean(yh * yh, axis=-1, keepdims=True) + EPS) * gnw_ref[...]
        cat_scr[:, off + h * GDN_HEAD_DIM:off + (h + 1) * GDN_HEAD_DIM] = (_silu(gg_ref[:, hs]) * yh).astype(BF16)
    o = _dot(cat_scr[...], w_ref[...])
    gate = _mod_row(gate_ref, tiles_per_batch)
    o_ref[...] = _ln(DEEPNORM_ALPHA * x_ref[...] + gate * o) * lng_ref[...] + lnb_ref[...]


def _outproj(x, mods_l, ys, prep, p_main, yr, og, dskip, snw, rnw, gnw, w_out, lng, lnb, tiles_per_batch):
    nt = x.shape[0]
    row = lambda width, col: pl.BlockSpec((TM_OUT, width), lambda i: (i, col))
    vec = lambda width: pl.BlockSpec((1, width), lambda i: (0, 0))
    return pl.pallas_call(
        functools.partial(_outproj_kernel, tiles_per_batch=tiles_per_batch),
        out_shape=jax.ShapeDtypeStruct((nt, D_MODEL), F32),
        grid=(nt // TM_OUT,),
        in_specs=[row(D_MODEL, 0),
                  pl.BlockSpec((SUBLANES, D_MODEL), lambda i: (0, 2)),
                  row(D_SSD, 0), row(D_SSD, 0), row(D_SSD, 0), row(D_SSD, COL_Z // D_SSD),
                  row(D_RET, 0), row(D_RET, 0), row(D_RET, COL_RET // 512 + 3),
                  row(D_GDN, 0), row(D_GDN, 0), row(D_GDN, COL_GG // 512),
                  vec(D_SSD), vec(D_SSD), vec(D_RET), vec(GDN_HEAD_DIM),
                  pl.BlockSpec((D_MODEL, D_MODEL), lambda i: (0, 0)),
                  vec(D_MODEL), vec(D_MODEL)],
        out_specs=row(D_MODEL, 0),
        scratch_shapes=[pltpu.VMEM((TM_OUT, D_MODEL), BF16)],
        compiler_params=_cparams(("arbitrary",)),
        name="outproj",
    )(x, mods_l, ys[0], ys[1], prep, p_main, yr[0], yr[1], p_main, og[0], og[1], p_main,
      dskip, snw, rnw, gnw, w_out, lng, lnb)


def _ffn_kernel(x_ref, sh_ref, sc_ref, gate_ref, w1_ref, w3_ref, w2_ref, lng_ref, lnb_ref, o_ref,
                h_scr, acc_scr, *, tiles_per_batch):
    f = pl.program_id(1)

    @pl.when(f == 0)
    def _():
        shift = _mod_row(sh_ref, tiles_per_batch)
        scale = _mod_row(sc_ref, tiles_per_batch)
        h_scr[...] = (_ln(x_ref[...]) * (1.0 + scale) + shift).astype(BF16)
        acc_scr[...] = jnp.zeros_like(acc_scr)

    h = h_scr[...]
    g = _silu(_dot(h, w1_ref[...])) * _dot(h, w3_ref[...])
    acc_scr[...] += _dot(g.astype(BF16), w2_ref[...])

    @pl.when(f == pl.num_programs(1) - 1)
    def _():
        gate = _mod_row(gate_ref, tiles_per_batch)
        o_ref[...] = _ln(DEEPNORM_ALPHA * x_ref[...] + gate * acc_scr[...]) * lng_ref[...] + lnb_ref[...]


def _ffn(x, mods_l, w1, w3, w2, lng, lnb, tiles_per_batch):
    nt = x.shape[0]
    mod = lambda k: pl.BlockSpec((SUBLANES, D_MODEL), lambda i, f: (0, k))
    vec = pl.BlockSpec((1, D_MODEL), lambda i, f: (0, 0))
    return pl.pallas_call(
        functools.partial(_ffn_kernel, tiles_per_batch=tiles_per_batch),
        out_shape=jax.ShapeDtypeStruct((nt, D_MODEL), F32),
        grid=(nt // TM, D_FF // TF),
        in_specs=[pl.BlockSpec((TM, D_MODEL), lambda i, f: (i, 0)), mod(3), mod(4), mod(5),
                  pl.BlockSpec((D_MODEL, TF), lambda i, f: (0, f)),
                  pl.BlockSpec((D_MODEL, TF), lambda i, f: (0, f)),
                  pl.BlockSpec((TF, D_MODEL), lambda i, f: (f, 0)),
                  vec, vec],
        out_specs=pl.BlockSpec((TM, D_MODEL), lambda i, f: (i, 0)),
        scratch_shapes=[pltpu.VMEM((TM, D_MODEL), BF16), pltpu.VMEM((TM, D_MODEL), F32)],
        compiler_params=_cparams(("arbitrary", "arbitrary")),
        name="ffn",
    )(x, mods_l, mods_l, mods_l, w1, w3, w2, lng, lnb)


def _top2_gates(logits):
    lane = lax.broadcasted_iota(jnp.int32, logits.shape, 1).astype(F32)
    neg = jnp.where(lane < N_EXPERTS, logits, -jnp.inf)
    m1 = jnp.max(neg, axis=-1, keepdims=True)
    i1 = jnp.min(jnp.where(neg == m1, lane, float(LANES)), axis=-1, keepdims=True)
    rest = jnp.where(lane == i1, -jnp.inf, neg)
    m2 = jnp.max(rest, axis=-1, keepdims=True)
    i2 = jnp.min(jnp.where(rest == m2, lane, float(LANES)), axis=-1, keepdims=True)
    e2 = jnp.exp(m2 - m1)
    inv = 1.0 / (1.0 + e2)
    return jnp.where(lane == i1, inv, 0.0) + jnp.where(lane == i2, e2 * inv, 0.0)


def _moe_kernel(x_ref, sh_ref, sc_ref, gate_ref, r_ref, w1_ref, w3_ref, w2_ref, lng_ref, lnb_ref,
                o_ref, h_scr, acc_scr, dg_scr, *, tiles_per_batch):
    e = pl.program_id(1)
    f = pl.program_id(2)

    @pl.when(jnp.logical_and(e == 0, f == 0))
    def _():
        shift = _mod_row(sh_ref, tiles_per_batch)
        scale = _mod_row(sc_ref, tiles_per_batch)
        h = _ln(x_ref[...]) * (1.0 + scale) + shift
        h_scr[...] = h.astype(BF16)
        acc_scr[...] = jnp.zeros_like(acc_scr)
        dg_scr[...] = _top2_gates(_dot_hi(h, r_ref[...]))

    h = h_scr[...]
    g = _silu(_dot(h, w1_ref[0])) * _dot(h, w3_ref[0])
    lane = lax.broadcasted_iota(jnp.int32, dg_scr.shape, 1)
    ge = jnp.sum(jnp.where(lane == e, dg_scr[...], 0.0), axis=-1, keepdims=True)
    acc_scr[...] += ge * _dot(g.astype(BF16), w2_ref[0])

    @pl.when(jnp.logical_and(e == pl.num_programs(1) - 1, f == pl.num_programs(2) - 1))
    def _():
        gate = _mod_row(gate_ref, tiles_per_batch)
        o_ref[...] = _ln(DEEPNORM_ALPHA * x_ref[...] + gate * acc_scr[...]) * lng_ref[...] + lnb_ref[...]


def _moe(x, mods_l, router, w1, w3, w2, lng, lnb, tiles_per_batch):
    nt = x.shape[0]
    mod = lambda k: pl.BlockSpec((SUBLANES, D_MODEL), lambda i, e, f: (0, k))
    vec = pl.BlockSpec((1, D_MODEL), lambda i, e, f: (0, 0))
    return pl.pallas_call(
        functools.partial(_moe_kernel, tiles_per_batch=tiles_per_batch),
        out_shape=jax.ShapeDtypeStruct((nt, D_MODEL), F32),
        grid=(nt // TM, N_EXPERTS, D_FF // TF),
        in_specs=[pl.BlockSpec((TM, D_MODEL), lambda i, e, f: (i, 0)), mod(3), mod(4), mod(5),
                  pl.BlockSpec((D_MODEL, LANES), lambda i, e, f: (0, 0)),
                  pl.BlockSpec((1, D_MODEL, TF), lambda i, e, f: (e, 0, f)),
                  pl.BlockSpec((1, D_MODEL, TF), lambda i, e, f: (e, 0, f)),
                  pl.BlockSpec((1, TF, D_MODEL), lambda i, e, f: (e, f, 0)),
                  vec, vec],
        out_specs=pl.BlockSpec((TM, D_MODEL), lambda i, e, f: (i, 0)),
        scratch_shapes=[pltpu.VMEM((TM, D_MODEL), BF16), pltpu.VMEM((TM, D_MODEL), F32),
                        pltpu.VMEM((TM, LANES), F32)],
        compiler_params=_cparams(("arbitrary", "arbitrary", "arbitrary")),
        name="moe",
    )(x, mods_l, mods_l, mods_l, router, w1, w3, w2, lng, lnb)


def _rope_tables(l):
    n_freq = RET_HEAD_DIM // 4
    inv_freq = ROPE_BASE ** (-jnp.arange(n_freq, dtype=F32) / n_freq)
    pos = jnp.arange(l, dtype=jnp.int32)
    row = (pos // GRID_W).astype(F32)
    col = (pos % GRID_W).astype(F32)
    ang = jnp.concatenate([row[:, None] * inv_freq, col[:, None] * inv_freq], axis=-1)
    cos, sin = jnp.cos(ang), jnp.sin(ang)
    cos_t = jnp.concatenate([cos, cos], axis=-1)
    sin_t = jnp.concatenate([-sin, sin], axis=-1)
    cos_t = jnp.concatenate([cos_t, jnp.ones((PREP_ROWS, LANES), F32)], axis=0)
    sin_t = jnp.concatenate([sin_t, jnp.zeros((PREP_ROWS, LANES), F32)], axis=0)
    return cos_t, sin_t


def _split_w_in(w):
    o_ret = D_SSD + SSD_XBC + 2 * SSD_HEADS
    o_gdn = o_ret + 4 * D_RET
    z = w[:, 0:D_SSD]
    xbc = w[:, D_SSD:D_SSD + SSD_XBC]
    dt = w[:, D_SSD + SSD_XBC:o_ret]
    ret = w[:, o_ret:o_gdn]
    gqkv = w[:, o_gdn:o_gdn + 3 * D_GDN]
    gg = w[:, o_gdn + 3 * D_GDN:o_gdn + 4 * D_GDN]
    ab = w[:, o_gdn + 4 * D_GDN:]
    main = jnp.concatenate([xbc, gqkv, z, ret, gg], axis=1).astype(BF16)
    small = jnp.concatenate([dt, ab, jnp.zeros((w.shape[0], LANES - dt.shape[1] - ab.shape[1]), w.dtype)], axis=1)
    return main, small.astype(BF16), small.T.astype(BF16)


def _lane_params(ssd_vals, gdn_vals):
    row = jnp.zeros((LANES,), F32)
    row = row.at[SM_DT:SM_DT + 2 * SSD_HEADS].set(ssd_vals.reshape(-1))
    row = row.at[SM_A:SM_A + 2 * GDN_HEADS].set(gdn_vals.reshape(-1))
    return row


def kernel(x, c, ctx, c_ctx, w_mod, b_mod, w_in, ssd_conv_w, ssd_conv_b, ssd_a_log, ssd_dt_bias, ssd_d, ssd_norm_w, ret_log_decay, ret_norm_w, gdn_conv_w, gdn_a_log, gdn_dt_bias, gdn_norm_w, w_out, ln1_g, ln1_b, ln2_g, ln2_b, ffn_w1, ffn_w3, ffn_w2, moe_router, moe_w1, moe_w3, moe_w2):
    b, l, d = x.shape
    lc = ctx.shape[1]
    assert d == D_MODEL and l % TM == 0 and (b * lc) % TM == 0 and lc % PREP_ROWS == 0 and l % GRID_W == 0
    tiles_per_batch = l // TM

    xt = jnp.concatenate([x.reshape(b * l, d), ctx.reshape(b * lc, d)], axis=0)
    mods = _mods(c, c_ctx, w_mod, b_mod)
    cos_t, sin_t = _rope_tables(l)

    for i in range(DEPTH):
        w_main, w_small, w_small_t = _split_w_in(w_in[i])
        conv_w = jnp.concatenate([ssd_conv_w[i], gdn_conv_w[i]], axis=1)
        conv_b = jnp.concatenate([ssd_conv_b[i], jnp.zeros((3 * D_GDN,), F32)])[None, :]
        bias_row = _lane_params(ssd_dt_bias[i], gdn_dt_bias[i])
        alog_row = _lane_params(ssd_a_log[i], gdn_a_log[i])
        prow = jnp.zeros((SUBLANES, LANES), F32).at[0].set(bias_row).at[1].set(alog_row)
        pcol = prow.T
        ld = jnp.broadcast_to(ret_log_decay[i].reshape(2 * RET_HEADS, 1), (2 * RET_HEADS, LANES))

        p_main, p_small, p_small_t = _inproj(xt, mods[i], w_main, w_small, w_small_t, tiles_per_batch)
        prep = _prep(p_main, cos_t, sin_t, conv_w, conv_b, b, l, lc)
        ys = _ssd_scan(prep, p_small, p_small_t, prow, pcol, b, l, lc)
        yr = _ret_scan(prep, p_main, ld, b, l, lc)
        og = _gdn_scan(prep, p_small, p_small_t, prow, pcol, b, l, lc)
        xt = _outproj(xt, mods[i], ys, prep, p_main, yr, og,
                      jnp.repeat(ssd_d[i], SSD_HEAD_DIM)[None, :], ssd_norm_w[i][None, :],
                      ret_norm_w[i][None, :], gdn_norm_w[i][None, :], w_out[i].astype(BF16),
                      ln1_g[i][None, :], ln1_b[i][None, :], l // TM_OUT)
        j = i // 2
        if i % 2 == 0:
            xt = _ffn(xt, mods[i], ffn_w1[j].astype(BF16), ffn_w3[j].astype(BF16), ffn_w2[j].astype(BF16),
                      ln2_g[i][None, :], ln2_b[i][None, :], tiles_per_batch)
        else:
            router = jnp.concatenate([moe_router[j], jnp.zeros((d, LANES - N_EXPERTS), F32)], axis=1)
            xt = _moe(xt, mods[i], router, moe_w1[j].astype(BF16), moe_w3[j].astype(BF16),
                      moe_w2[j].astype(BF16), ln2_g[i][None, :], ln2_b[i][None, :], tiles_per_batch)
    return xt[:b * l].reshape(b, l, d)
```

```python
import functools

import jax
import jax.numpy as jnp
from jax import lax
from jax.experimental import pallas as pl
from jax.experimental.pallas import tpu as pltpu

F32 = jnp.float32
BF16 = jnp.bfloat16

D_MODEL = 2048
DEPTH = 4
GRID_W = 64
D_SSD = 1024
SSD_HEAD_DIM = 64
SSD_HEADS = 16
SSD_GROUPS = 2
SSD_STATE = 128
SSD_XBC = D_SSD + 2 * SSD_GROUPS * SSD_STATE
D_RET = 512
RET_HEAD_DIM = 128
RET_HEADS = 4
D_GDN = 512
GDN_HEAD_DIM = 128
GDN_HEADS = 4
CONV_W = 5
ROPE_BASE = 10000.0
D_FF = 5632
N_EXPERTS = 8
DEEPNORM_ALPHA = (2 * DEPTH) ** 0.25
EPS = 1e-6

LANES = 128
SUBLANES = 8
VMEM_LIMIT = 56 * 1024 * 1024

TM = 512
TM_OUT = 256
TMM = 512
PREP_ROWS = 256
CHUNK = 128
TN_IN = 512
TF = 512
TN_MOD = 1024

COL_CONV = 0
COL_Z = 3072
COL_RET = 4096
COL_GG = 6144
N_MAIN = 6656
SM_DT = 0
SM_A = 32
SM_B = 40


def _silu(x):
    return x * jax.nn.sigmoid(x)


def _softplus(x):
    return jnp.maximum(x, 0.0) + jnp.log(1.0 + jnp.exp(-jnp.abs(x)))


def _ln(x):
    mu = jnp.mean(x, axis=-1, keepdims=True)
    xc = x - mu
    var = jnp.mean(xc * xc, axis=-1, keepdims=True)
    return xc * lax.rsqrt(var + EPS)


def _dot(a, b):
    return jnp.dot(a, b, preferred_element_type=F32)


def _dot_nt(a, b):
    return lax.dot_general(a, b, (((1,), (1,)), ((), ())), preferred_element_type=F32)


def _dot_tn(a, b):
    return lax.dot_general(a, b, (((0,), (0,)), ((), ())), preferred_element_type=F32)


def _split2(x):
    hi = x.astype(BF16)
    lo = (x - hi.astype(F32)).astype(BF16)
    return hi, lo


def _split3(x):
    hi = x.astype(BF16)
    r = x - hi.astype(F32)
    mid = r.astype(BF16)
    lo = (r - mid.astype(F32)).astype(BF16)
    return hi, mid, lo


def _dot_hi(a, b):
    ah, al = _split2(a)
    bh, bl = _split2(b)
    return _dot(ah, bh) + _dot(ah, bl) + _dot(al, bh)


def _dot01_lhs(m01, a):
    hi, mid, lo = _split3(a)
    return _dot(m01, hi) + _dot(m01, mid) + _dot(m01, lo)


def _dot01_rhs(a, m01):
    hi, mid, lo = _split3(a)
    return _dot(hi, m01) + _dot(mid, m01) + _dot(lo, m01)


def _cparams(sem):
    return pltpu.CompilerParams(dimension_semantics=sem, vmem_limit_bytes=VMEM_LIMIT)


def _mods_kernel(cb_ref, w_ref, b_ref, o_ref):
    tn = w_ref.shape[2]
    rows = []
    for r in range(3):
        act = _silu(cb_ref[r])
        pieces = [jnp.sum(w_ref[0, :, j * LANES:(j + 1) * LANES] * act, axis=0, keepdims=True)
                  for j in range(tn // LANES)]
        rows.append(jnp.concatenate(pieces, axis=1) + b_ref[0])
    rows.append(jnp.zeros((SUBLANES - 3, tn), F32))
    o_ref[0] = jnp.concatenate(rows, axis=0)


def _mods(c, c_ctx, w_mod, b_mod):
    cvec = jnp.concatenate([c, c_ctx[None, :]], axis=0)
    cb = jnp.broadcast_to(cvec[:, :, None], (3, D_MODEL, LANES))
    n = w_mod.shape[2]
    return pl.pallas_call(
        _mods_kernel,
        out_shape=jax.ShapeDtypeStruct((DEPTH, SUBLANES, n), F32),
        grid=(DEPTH, n // TN_MOD),
        in_specs=[pl.BlockSpec((3, D_MODEL, LANES), lambda l, j: (0, 0, 0)),
                  pl.BlockSpec((1, D_MODEL, TN_MOD), lambda l, j: (l, 0, j)),
                  pl.BlockSpec((1, 1, TN_MOD), lambda l, j: (l, 0, j))],
        out_specs=pl.BlockSpec((1, SUBLANES, TN_MOD), lambda l, j: (l, 0, j)),
        compiler_params=_cparams(("arbitrary", "arbitrary")),
        name="mods",
    )(cb, w_mod, b_mod.reshape(DEPTH, 1, n))


def _mod_row(ref, tiles_per_batch):
    s = jnp.minimum(pl.program_id(0) // tiles_per_batch, 2)
    return ref[pl.ds(s, 1), :]


def _inproj_kernel(x_ref, sh_ref, sc_ref, w_ref, ws_ref, wst_ref, p_ref, ps_ref, pst_ref, h_scr,
                   *, tiles_per_batch):
    @pl.when(pl.program_id(1) == 0)
    def _():
        shift = _mod_row(sh_ref, tiles_per_batch)
        scale = _mod_row(sc_ref, tiles_per_batch)
        hb = (_ln(x_ref[...]) * (1.0 + scale) + shift).astype(BF16)
        h_scr[...] = hb
        ps_ref[...] = _dot(hb, ws_ref[...])
        pst_ref[...] = _dot_nt(wst_ref[...], hb)

    p_ref[...] = _dot(h_scr[...], w_ref[...])


def _inproj(x, mods_l, w_main, w_small, w_small_t, tiles_per_batch):
    nt = x.shape[0]
    return pl.pallas_call(
        functools.partial(_inproj_kernel, tiles_per_batch=tiles_per_batch),
        out_shape=(jax.ShapeDtypeStruct((nt, N_MAIN), F32),
                   jax.ShapeDtypeStruct((nt, LANES), F32),
                   jax.ShapeDtypeStruct((LANES, nt), F32)),
        grid=(nt // TM, N_MAIN // TN_IN),
        in_specs=[pl.BlockSpec((TM, D_MODEL), lambda i, j: (i, 0)),
                  pl.BlockSpec((SUBLANES, D_MODEL), lambda i, j: (0, 0)),
                  pl.BlockSpec((SUBLANES, D_MODEL), lambda i, j: (0, 1)),
                  pl.BlockSpec((D_MODEL, TN_IN), lambda i, j: (0, j)),
                  pl.BlockSpec((D_MODEL, LANES), lambda i, j: (0, 0)),
                  pl.BlockSpec((LANES, D_MODEL), lambda i, j: (0, 0))],
        out_specs=(pl.BlockSpec((TM, TN_IN), lambda i, j: (i, j)),
                   pl.BlockSpec((TM, LANES), lambda i, j: (i, 0)),
                   pl.BlockSpec((LANES, TM), lambda i, j: (0, i))),
        scratch_shapes=[pltpu.VMEM((TM, D_MODEL), BF16)],
        compiler_params=_cparams(("arbitrary", "arbitrary")),
        name="inproj",
    )(x, mods_l, mods_l, w_main, w_small, w_small_t)


def _prep_kernel(main_ref, prev_ref, next_ref, rq_ref, rk_ref, cos_ref, sin_ref, cw_ref, cb_ref,
                 o_ref, ext_scr, *, seg_starts, seg_ends):
    i = pl.program_id(0)
    r = main_ref.shape[0]
    is_start = functools.reduce(jnp.logical_or, [i == s for s in seg_starts])
    is_end = functools.reduce(jnp.logical_or, [i == s for s in seg_ends])
    ext_scr[0:SUBLANES, :] = jnp.where(is_start, 0.0, prev_ref[...])
    ext_scr[SUBLANES:SUBLANES + r, :] = main_ref[...]
    ext_scr[SUBLANES + r:2 * SUBLANES + r, :] = jnp.where(is_end, 0.0, next_ref[...])

    half = CONV_W // 2
    n_conv = main_ref.shape[1]
    for c0 in range(0, n_conv, 512):
        cs = slice(c0, c0 + 512)
        acc = jnp.broadcast_to(cb_ref[:, cs], (r, 512))
        for j in range(CONV_W):
            acc = acc + ext_scr[SUBLANES - half + j:SUBLANES - half + j + r, cs] * cw_ref[j:j + 1, cs]
        act = _silu(acc)
        if c0 in (1536, 2048):
            scale = GDN_HEAD_DIM ** -0.5 if c0 == 1536 else 1.0
            for h in range(GDN_HEADS):
                xh = act[:, h * LANES:(h + 1) * LANES]
                inv = lax.rsqrt(jnp.sum(xh * xh, axis=-1, keepdims=True) + EPS)
                o_ref[:, c0 + h * LANES:c0 + (h + 1) * LANES] = xh * inv * scale
        else:
            o_ref[:, cs] = act

    cosf = cos_ref[...]
    sinf = sin_ref[...]
    for src, off, scale in ((rq_ref, 3072, 1.0), (rk_ref, 3584, RET_HEAD_DIM ** -0.5)):
        for h in range(RET_HEADS):
            xh = src[:, h * LANES:(h + 1) * LANES]
            rot = xh * cosf + pltpu.roll(xh, LANES // 2, axis=1) * sinf
            o_ref[:, off + h * LANES:off + (h + 1) * LANES] = rot * scale


def _prep(p_main, cos_t, sin_t, conv_w, conv_b, b, l, lc):
    nt = p_main.shape[0]
    r = PREP_ROWS
    n_chunks = nt // r
    lat_chunks = (b * l) // r
    seg_rows = [k * l for k in range(b)] + [b * l + k * lc for k in range(b)]
    seg_len = [l] * b + [lc] * b
    seg_starts = tuple(s // r for s in seg_rows)
    seg_ends = tuple((s + n) // r - 1 for s, n in zip(seg_rows, seg_len))
    sub_per = r // SUBLANES
    n_sub = nt // SUBLANES
    rope_map = lambda i: (jnp.where(i < lat_chunks, i % (l // r), l // r), 0)
    return pl.pallas_call(
        functools.partial(_prep_kernel, seg_starts=seg_starts, seg_ends=seg_ends),
        out_shape=jax.ShapeDtypeStruct((nt, 4096), F32),
        grid=(n_chunks,),
        in_specs=[pl.BlockSpec((r, 3072), lambda i: (i, 0)),
                  pl.BlockSpec((SUBLANES, 3072), lambda i: (jnp.maximum(i * sub_per - 1, 0), 0)),
                  pl.BlockSpec((SUBLANES, 3072), lambda i: (jnp.minimum((i + 1) * sub_per, n_sub - 1), 0)),
                  pl.BlockSpec((r, 512), lambda i: (i, COL_RET // 512)),
                  pl.BlockSpec((r, 512), lambda i: (i, COL_RET // 512 + 1)),
                  pl.BlockSpec((r, LANES), rope_map),
                  pl.BlockSpec((r, LANES), rope_map),
                  pl.BlockSpec((CONV_W, 3072), lambda i: (0, 0)),
                  pl.BlockSpec((1, 3072), lambda i: (0, 0))],
        out_specs=pl.BlockSpec((r, 4096), lambda i: (i, 0)),
        scratch_shapes=[pltpu.VMEM((r + 2 * SUBLANES, 3072), F32)],
        compiler_params=_cparams(("arbitrary",)),
        name="prep",
    )(p_main, p_main, p_main, p_main, p_main, cos_t, sin_t, conv_w, conv_b)


def _chunk_maps(b, l, lc):
    ncc, ncl = lc // CHUNK, l // CHUNK
    ctx0 = (b * l) // CHUNK

    def fwd(bi, s):
        return jnp.where(s < ncc, ctx0 + bi * ncc + s, bi * ncl + s - ncc)

    def bwd(bi, s):
        return jnp.where(s < ncc, ctx0 + bi * ncc + (ncc - 1 - s), bi * ncl + (ncl - 1 - (s - ncc)))

    return (fwd, bwd), ncc + ncl


def _tri_masks():
    tt = lax.broadcasted_iota(jnp.int32, (CHUNK, CHUNK), 0)
    ss = lax.broadcasted_iota(jnp.int32, (CHUNK, CHUNK), 1)
    return ss <= tt, ss >= tt


def _cumsums(la_c, la_r, d, low, upp):
    lowf = jnp.where(low, 1.0, 0.0).astype(BF16)
    uppf = jnp.where(upp, 1.0, 0.0).astype(BF16)
    if d == 0:
        return _dot01_lhs(lowf, la_c), _dot01_rhs(la_r, uppf), low
    return _dot01_lhs(uppf, la_c), _dot01_rhs(la_r, lowf), upp


def _ssd_kernel(xs_f, bc_f, sm_f, smt_f, xs_b, bc_b, sm_b, smt_b, prow_ref, pcol_ref,
                y_f, y_b, s_scr):
    @pl.when(pl.program_id(1) == 0)
    def _():
        s_scr[...] = jnp.zeros_like(s_scr)

    low, upp = _tri_masks()
    bias_r, alog_r = prow_ref[0:1, :], prow_ref[1:2, :]
    bias_c, alog_c = pcol_ref[:, 0:1], pcol_ref[:, 1:2]
    heads_per_group = SSD_HEADS // SSD_GROUPS
    group_w = heads_per_group * SSD_HEAD_DIM
    groups = []
    for d, (xs_ref, bc_ref, sm_ref, smt_ref, y_ref) in enumerate(
            ((xs_f, bc_f, sm_f, smt_f, y_f), (xs_b, bc_b, sm_b, smt_b, y_b))):
        dt_c = _softplus(sm_ref[...] + bias_r)
        dt_r = _softplus(smt_ref[...] + bias_c)
        g_c, g_r, mask = _cumsums(-dt_c * jnp.exp(alog_r), -dt_r * jnp.exp(alog_c), d, low, upp)
        last = CHUNK - 1 if d == 0 else 0
        for g in range(SSD_GROUPS):
            gs = slice(g * group_w, (g + 1) * group_w)
            kg = bc_ref[:, g * SSD_STATE:(g + 1) * SSD_STATE].astype(BF16)
            qg = bc_ref[:, (SSD_GROUPS + g) * SSD_STATE:(SSD_GROUPS + g + 1) * SSD_STATE].astype(BF16)
            qk = _dot_nt(qg, kg)
            heads = []
            for hh in range(heads_per_group):
                h = g * heads_per_group + hh
                ln = SM_DT + d * SSD_HEADS + h
                gcol = g_c[:, ln:ln + 1]
                grow = g_r[ln:ln + 1, :]
                glast = g_c[last:last + 1, ln:ln + 1]
                decay = jnp.exp(jnp.where(mask, gcol - grow, -jnp.inf))
                v = xs_ref[:, h * SSD_HEAD_DIM:(h + 1) * SSD_HEAD_DIM] * dt_c[:, ln:ln + 1]
                heads.append(dict(p=(qk * decay).astype(BF16), v16=v.astype(BF16), into=jnp.exp(gcol),
                                  vdec=v * jnp.exp(glast - gcol),
                                  carry=jnp.broadcast_to(jnp.exp(glast), (1, SSD_HEAD_DIM))))
            groups.append(dict(d=d, gs=gs, y_ref=y_ref, kg=kg, qg=qg, state=s_scr[d, :, gs], heads=heads))

    for w in groups:
        w["inter"] = _dot(w["qg"], w["state"].astype(BF16))
        for hd in w["heads"]:
            hd["intra"] = _dot(hd["p"], hd["v16"])
    for w in groups:
        y = [hd["intra"] + hd["into"] * w["inter"][:, hh * SSD_HEAD_DIM:(hh + 1) * SSD_HEAD_DIM]
             for hh, hd in enumerate(w["heads"])]
        w["y_ref"][:, w["gs"]] = jnp.concatenate(y, axis=1)
        vdec = jnp.concatenate([hd["vdec"] for hd in w["heads"]], axis=1).astype(BF16)
        carry = jnp.concatenate([hd["carry"] for hd in w["heads"]], axis=1)
        s_scr[w["d"], :, w["gs"]] = w["state"] * carry + _dot_tn(w["kg"], vdec)


def _ssd_scan(prep, p_small, p_small_t, prow, pcol, b, l, lc):
    nt = prep.shape[0]
    maps, steps = _chunk_maps(b, l, lc)
    in_specs = []
    for m in maps:
        in_specs += [pl.BlockSpec((CHUNK, D_SSD), lambda bi, s, m=m: (m(bi, s), 0)),
                     pl.BlockSpec((CHUNK, 512), lambda bi, s, m=m: (m(bi, s), 2)),
                     pl.BlockSpec((CHUNK, LANES), lambda bi, s, m=m: (m(bi, s), 0)),
                     pl.BlockSpec((LANES, CHUNK), lambda bi, s, m=m: (0, m(bi, s)))]
    in_specs += [pl.BlockSpec((SUBLANES, LANES), lambda bi, s: (0, 0)),
                 pl.BlockSpec((LANES, SUBLANES), lambda bi, s: (0, 0))]
    return pl.pallas_call(
        _ssd_kernel,
        out_shape=(jax.ShapeDtypeStruct((nt, D_SSD), F32),) * 2,
        grid=(b, steps),
        in_specs=in_specs,
        out_specs=tuple(pl.BlockSpec((CHUNK, D_SSD), lambda bi, s, m=m: (m(bi, s), 0)) for m in maps),
        scratch_shapes=[pltpu.VMEM((2, SSD_STATE, D_SSD), F32)],
        compiler_params=_cparams(("arbitrary", "arbitrary")),
        name="ssd_scan",
    )(prep, prep, p_small, p_small_t, prep, prep, p_small, p_small_t, prow, pcol)


def _ret_kernel(q_f, k_f, v_f, q_b, k_b, v_b, ld_ref, y_f, y_b, s_scr):
    @pl.when(pl.program_id(1) == 0)
    def _():
        s_scr[...] = jnp.zeros_like(s_scr)

    low, upp = _tri_masks()
    tt = lax.broadcasted_iota(jnp.int32, (CHUNK, CHUNK), 0)
    ss = lax.broadcasted_iota(jnp.int32, (CHUNK, CHUNK), 1)
    tcol = lax.broadcasted_iota(jnp.int32, (CHUNK, 1), 0)
    for d, (q_ref, k_ref, v_ref, y_ref) in enumerate(((q_f, k_f, v_f, y_f), (q_b, k_b, v_b, y_b))):
        mask = low if d == 0 else upp
        dist = (tt - ss if d == 0 else ss - tt).astype(F32)
        n_in = (tcol + 1 if d == 0 else CHUNK - tcol).astype(F32)
        n_out = (CHUNK - 1 - tcol if d == 0 else tcol).astype(F32)
        for h in range(RET_HEADS):
            hs = slice(h * RET_HEAD_DIM, (h + 1) * RET_HEAD_DIM)
            ld = ld_ref[d * RET_HEADS + h:d * RET_HEADS + h + 1, :]
            ld1 = ld[:, 0:1]
            decay = jnp.exp(jnp.where(mask, dist * ld, -jnp.inf))
            q = q_ref[:, hs].astype(BF16)
            k = k_ref[:, hs]
            v = v_ref[:, hs]
            state = s_scr[d, h]
            y_ref[:, hs] = (_dot((_dot_nt(q, k.astype(BF16)) * decay).astype(BF16), v.astype(BF16))
                            + jnp.exp(n_in * ld1) * _dot(q, state.astype(BF16)))
            kdec = (k * jnp.exp(n_out * ld1)).astype(BF16)
            s_scr[d, h] = state * jnp.exp(CHUNK * ld1) + _dot_tn(kdec, v.astype(BF16))


def _ret_scan(prep, p_main, ld, b, l, lc):
    nt = prep.shape[0]
    maps, steps = _chunk_maps(b, l, lc)
    in_specs = []
    for m in maps:
        in_specs += [pl.BlockSpec((CHUNK, D_RET), lambda bi, s, m=m: (m(bi, s), 6)),
                     pl.BlockSpec((CHUNK, D_RET), lambda bi, s, m=m: (m(bi, s), 7)),
                     pl.BlockSpec((CHUNK, D_RET), lambda bi, s, m=m: (m(bi, s), COL_RET // 512 + 2))]
    in_specs += [pl.BlockSpec((SUBLANES, LANES), lambda bi, s: (0, 0))]
    return pl.pallas_call(
        _ret_kernel,
        out_shape=(jax.ShapeDtypeStruct((nt, D_RET), F32),) * 2,
        grid=(b, steps),
        in_specs=in_specs,
        out_specs=tuple(pl.BlockSpec((CHUNK, D_RET), lambda bi, s, m=m: (m(bi, s), 0)) for m in maps),
        scratch_shapes=[pltpu.VMEM((2, RET_HEADS, RET_HEAD_DIM, RET_HEAD_DIM), F32)],
        compiler_params=_cparams(("arbitrary", "arbitrary")),
        name="ret_scan",
    )(prep, prep, p_main, prep, prep, p_main, ld)


def _gdn_kernel(q_f, k_f, v_f, sm_f, smt_f, q_b, k_b, v_b, sm_b, smt_b, prow_ref, pcol_ref,
                o_f, o_b, s_scr):
    @pl.when(pl.program_id(1) == 0)
    def _():
        s_scr[...] = jnp.zeros_like(s_scr)

    low, upp = _tri_masks()
    bias_r, alog_r = prow_ref[0:1, :], prow_ref[1:2, :]
    bias_c, alog_c = pcol_ref[:, 0:1], pcol_ref[:, 1:2]
    n_doublings = CHUNK.bit_length() - 1
    tt = lax.broadcasted_iota(jnp.int32, (CHUNK, CHUNK), 0)
    ss = lax.broadcasted_iota(jnp.int32, (CHUNK, CHUNK), 1)
    chains = []
    for d, (q_ref, k_ref, v_ref, sm_ref, smt_ref, o_ref) in enumerate(
            ((q_f, k_f, v_f, sm_f, smt_f, o_f), (q_b, k_b, v_b, sm_b, smt_b, o_b))):
        sm = sm_ref[...]
        la_c = -jnp.exp(alog_r) * _softplus(sm + bias_r)
        la_r = -jnp.exp(alog_c) * _softplus(smt_ref[...] + bias_c)
        beta_c = jax.nn.sigmoid(sm)
        g_c, g_r, mask = _cumsums(la_c, la_r, d, low, upp)
        strict = jnp.logical_and(low, jnp.logical_not(upp)) if d == 0 else jnp.logical_and(upp, jnp.logical_not(low))
        last = CHUNK - 1 if d == 0 else 0
        for h in range(GDN_HEADS):
            hs = slice(h * GDN_HEAD_DIM, (h + 1) * GDN_HEAD_DIM)
            la_ln = SM_A + d * GDN_HEADS + h
            b_ln = SM_B + d * GDN_HEADS + h
            gcol = g_c[:, la_ln:la_ln + 1]
            grow = g_r[la_ln:la_ln + 1, :]
            glast = g_c[last:last + 1, la_ln:la_ln + 1]
            beta = beta_c[:, b_ln:b_ln + 1]
            incl = jnp.exp(jnp.where(mask, gcol - grow, -jnp.inf))
            q = q_ref[:, hs]
            k = k_ref[:, hs]
            v = v_ref[:, hs]
            kb = k * beta
            k16 = k.astype(BF16)
            chains.append(dict(
                d=d, h=h, hs=hs, o_ref=o_ref,
                a_mat=jnp.where(strict, _dot_nt(kb.astype(BF16), k16) * incl, 0.0),
                rhs=jnp.concatenate([v * beta, kb * jnp.exp(gcol)], axis=1).astype(BF16),
                attn=(_dot_nt(q.astype(BF16), k16) * incl).astype(BF16),
                qdec=(q * jnp.exp(gcol)).astype(BF16),
                kdec=(k * jnp.exp(glast - gcol)).astype(BF16),
                carry=jnp.exp(glast)))

    eye = jnp.where(low & upp, 1.0, 0.0)
    for j in range(n_doublings):
        s = 1 << j
        same = (tt >> (j + 1)) == (ss >> (j + 1))
        t_hi, s_hi = (tt & s) != 0, (ss & s) != 0
        join = (same & t_hi & ~s_hi, same & ~t_hi & s_hi)
        if j == 0:
            for c in chains:
                c["inv"] = eye - jnp.where(join[c["d"]], c["a_mat"], 0.0)
            continue
        for c in chains:
            c["inv16"] = c["inv"].astype(BF16)
            c["tmp"] = _dot(jnp.where(join[c["d"]], c["a_mat"], 0.0).astype(BF16), c["inv16"]).astype(BF16)
        for c in chains:
            c["inv"] = c["inv"] - _dot(c["inv16"], c["tmp"])
    for c in chains:
        c["sol"] = _dot(c["inv"].astype(BF16), c["rhs"])
        c["state"] = s_scr[c["d"], c["h"]]
        c["s16"] = c["state"].astype(BF16)
    for c in chains:
        v_new = c["sol"][:, :GDN_HEAD_DIM] - _dot(c["sol"][:, GDN_HEAD_DIM:].astype(BF16), c["s16"])
        c["vn16"] = v_new.astype(BF16)
        c["qs"] = _dot(c["qdec"], c["s16"])
    for c in chains:
        c["o_ref"][:, c["hs"]] = c["qs"] + _dot(c["attn"], c["vn16"])
        s_scr[c["d"], c["h"]] = c["state"] * c["carry"] + _dot_tn(c["kdec"], c["vn16"])


def _gdn_scan(prep, p_small, p_small_t, prow, pcol, b, l, lc):
    nt = prep.shape[0]
    maps, steps = _chunk_maps(b, l, lc)
    in_specs = []
    for m in maps:
        in_specs += [pl.BlockSpec((CHUNK, D_GDN), lambda bi, s, m=m: (m(bi, s), 3)),
                     pl.BlockSpec((CHUNK, D_GDN), lambda bi, s, m=m: (m(bi, s), 4)),
                     pl.BlockSpec((CHUNK, D_GDN), lambda bi, s, m=m: (m(bi, s), 5)),
                     pl.BlockSpec((CHUNK, LANES), lambda bi, s, m=m: (m(bi, s), 0)),
                     pl.BlockSpec((LANES, CHUNK), lambda bi, s, m=m: (0, m(bi, s)))]
    in_specs += [pl.BlockSpec((SUBLANES, LANES), lambda bi, s: (0, 0)),
                 pl.BlockSpec((LANES, SUBLANES), lambda bi, s: (0, 0))]
    return pl.pallas_call(
        _gdn_kernel,
        out_shape=(jax.ShapeDtypeStruct((nt, D_GDN), F32),) * 2,
        grid=(b, steps),
        in_specs=in_specs,
        out_specs=tuple(pl.BlockSpec((CHUNK, D_GDN), lambda bi, s, m=m: (m(bi, s), 0)) for m in maps),
        scratch_shapes=[pltpu.VMEM((2, GDN_HEADS, GDN_HEAD_DIM, GDN_HEAD_DIM), F32)],
        compiler_params=_cparams(("arbitrary", "arbitrary")),
        name="gdn_scan",
    )(prep, prep, prep, p_small, p_small_t, prep, prep, prep, p_small, p_small_t, prow, pcol)


def _outproj_kernel(x_ref, gate_ref, ysf, ysb, xs_ref, z_ref, yrf, yrb, rg_ref, ogf, ogb, gg_ref,
                    dskip_ref, snw_ref, rnw_ref, gnw_ref, w_ref, lng_ref, lnb_ref, o_ref, cat_scr,
                    *, tiles_per_batch):
    y = ysf[...] + ysb[...] + dskip_ref[...] * xs_ref[...]
    y = y * _silu(z_ref[...])
    y = y * lax.rsqrt(jnp.mean(y * y, axis=-1, keepdims=True) + EPS)
    cat_scr[:, 0:D_SSD] = (y * snw_ref[...]).astype(BF16)
    for h in range(RET_HEADS):
        hs = slice(h * RET_HEAD_DIM, (h + 1) * RET_HEAD_DIM)
        yh = _ln(yrf[:, hs] + yrb[:, hs]) * rnw_ref[:, hs]
        cat_scr[:, D_SSD + h * RET_HEAD_DIM:D_SSD + (h + 1) * RET_HEAD_DIM] = (_silu(rg_ref[:, hs]) * yh).astype(BF16)
    off = D_SSD + D_RET
    for h in range(GDN_HEADS):
        hs = slice(h * GDN_HEAD_DIM, (h + 1) * GDN_HEAD_DIM)
        yh = ogf[:, hs] + ogb[:, hs]
        yh = yh * lax.rsqrt(jnp.mean(yh * yh, axis=-1, keepdims=True) + EPS) * gnw_ref[...]
        cat_scr[:, off + h * GDN_HEAD_DIM:off + (h + 1) * GDN_HEAD_DIM] = (_silu(gg_ref[:, hs]) * yh).astype(BF16)
    o = _dot(cat_scr[...], w_ref[...])
    gate = _mod_row(gate_ref, tiles_per_batch)
    o_ref[...] = _ln(DEEPNORM_ALPHA * x_ref[...] + gate * o) * lng_ref[...] + lnb_ref[...]


def _outproj(x, mods_l, ys, prep, p_main, yr, og, dskip, snw, rnw, gnw, w_out, lng, lnb, tiles_per_batch):
    nt = x.shape[0]
    row = lambda width, col: pl.BlockSpec((TM_OUT, width), lambda i: (i, col))
    vec = lambda width: pl.BlockSpec((1, width), lambda i: (0, 0))
    return pl.pallas_call(
        functools.partial(_outproj_kernel, tiles_per_batch=tiles_per_batch),
        out_shape=jax.ShapeDtypeStruct((nt, D_MODEL), F32),
        grid=(nt // TM_OUT,),
        in_specs=[row(D_MODEL, 0),
                  pl.BlockSpec((SUBLANES, D_MODEL), lambda i: (0, 2)),
                  row(D_SSD, 0), row(D_SSD, 0), row(D_SSD, 0), row(D_SSD, COL_Z // D_SSD),
                  row(D_RET, 0), row(D_RET, 0), row(D_RET, COL_RET // 512 + 3),
                  row(D_GDN, 0), row(D_GDN, 0), row(D_GDN, COL_GG // 512),
                  vec(D_SSD), vec(D_SSD), vec(D_RET), vec(GDN_HEAD_DIM),
                  pl.BlockSpec((D_MODEL, D_MODEL), lambda i: (0, 0)),
                  vec(D_MODEL), vec(D_MODEL)],
        out_specs=row(D_MODEL, 0),
        scratch_shapes=[pltpu.VMEM((TM_OUT, D_MODEL), BF16)],
        compiler_params=_cparams(("arbitrary",)),
        name="outproj",
    )(x, mods_l, ys[0], ys[1], prep, p_main, yr[0], yr[1], p_main, og[0], og[1], p_main,
      dskip, snw, rnw, gnw, w_out, lng, lnb)


def _ffn_kernel(x_ref, sh_ref, sc_ref, gate_ref, w1_ref, w3_ref, w2_ref, lng_ref, lnb_ref, o_ref,
                h_scr, acc_scr, *, tiles_per_batch):
    f = pl.program_id(1)

    @pl.when(f == 0)
    def _():
        shift = _mod_row(sh_ref, tiles_per_batch)
        scale = _mod_row(sc_ref, tiles_per_batch)
        h_scr[...] = (_ln(x_ref[...]) * (1.0 + scale) + shift).astype(BF16)
        acc_scr[...] = jnp.zeros_like(acc_scr)

    h = h_scr[...]
    g = _silu(_dot(h, w1_ref[...])) * _dot(h, w3_ref[...])
    acc_scr[...] += _dot(g.astype(BF16), w2_ref[...])

    @pl.when(f == pl.num_programs(1) - 1)
    def _():
        gate = _mod_row(gate_ref, tiles_per_batch)
        o_ref[...] = _ln(DEEPNORM_ALPHA * x_ref[...] + gate * acc_scr[...]) * lng_ref[...] + lnb_ref[...]


def _ffn(x, mods_l, w1, w3, w2, lng, lnb, tiles_per_batch):
    nt = x.shape[0]
    mod = lambda k: pl.BlockSpec((SUBLANES, D_MODEL), lambda i, f: (0, k))
    vec = pl.BlockSpec((1, D_MODEL), lambda i, f: (0, 0))
    return pl.pallas_call(
        functools.partial(_ffn_kernel, tiles_per_batch=tiles_per_batch),
        out_shape=jax.ShapeDtypeStruct((nt, D_MODEL), F32),
        grid=(nt // TM, D_FF // TF),
        in_specs=[pl.BlockSpec((TM, D_MODEL), lambda i, f: (i, 0)), mod(3), mod(4), mod(5),
                  pl.BlockSpec((D_MODEL, TF), lambda i, f: (0, f)),
                  pl.BlockSpec((D_MODEL, TF), lambda i, f: (0, f)),
                  pl.BlockSpec((TF, D_MODEL), lambda i, f: (f, 0)),
                  vec, vec],
        out_specs=pl.BlockSpec((TM, D_MODEL), lambda i, f: (i, 0)),
        scratch_shapes=[pltpu.VMEM((TM, D_MODEL), BF16), pltpu.VMEM((TM, D_MODEL), F32)],
        compiler_params=_cparams(("arbitrary", "arbitrary")),
        name="ffn",
    )(x, mods_l, mods_l, mods_l, w1, w3, w2, lng, lnb)


def _top2(logits):
    lane = lax.broadcasted_iota(jnp.int32, logits.shape, 1).astype(F32)
    neg = jnp.where(lane < N_EXPERTS, logits, -jnp.inf)
    m1 = jnp.max(neg, axis=-1, keepdims=True)
    i1 = jnp.min(jnp.where(neg == m1, lane, float(LANES)), axis=-1, keepdims=True)
    rest = jnp.where(lane == i1, -jnp.inf, neg)
    m2 = jnp.max(rest, axis=-1, keepdims=True)
    i2 = jnp.min(jnp.where(rest == m2, lane, float(LANES)), axis=-1, keepdims=True)
    e2 = jnp.exp(m2 - m1)
    inv = 1.0 / (1.0 + e2)
    return (jnp.where(lane == 0.0, i1, 0.0) + jnp.where(lane == 1.0, i2, 0.0)
            + jnp.where(lane == 2.0, inv, 0.0) + jnp.where(lane == 3.0, e2 * inv, 0.0))


def _route_kernel(x_ref, sh_ref, sc_ref, r_ref, h_ref, info_ref, *, tiles_per_batch):
    shift = _mod_row(sh_ref, tiles_per_batch)
    scale = _mod_row(sc_ref, tiles_per_batch)
    h = _ln(x_ref[...]) * (1.0 + scale) + shift
    h_ref[...] = h
    info_ref[...] = _top2(_dot_hi(h, r_ref[...]))


def _route(x, mods_l, router, tiles_per_batch):
    nt = x.shape[0]
    mod = lambda k: pl.BlockSpec((SUBLANES, D_MODEL), lambda i: (0, k))
    return pl.pallas_call(
        functools.partial(_route_kernel, tiles_per_batch=tiles_per_batch),
        out_shape=(jax.ShapeDtypeStruct((nt, D_MODEL), F32), jax.ShapeDtypeStruct((nt, LANES), F32)),
        grid=(nt // TM,),
        in_specs=[pl.BlockSpec((TM, D_MODEL), lambda i: (i, 0)), mod(3), mod(4),
                  pl.BlockSpec((D_MODEL, LANES), lambda i: (0, 0))],
        out_specs=(pl.BlockSpec((TM, D_MODEL), lambda i: (i, 0)), pl.BlockSpec((TM, LANES), lambda i: (i, 0))),
        compiler_params=_cparams(("arbitrary",)),
        name="route",
    )(x, mods_l, mods_l, router)


def _dispatch(info):
    nt = info.shape[0]
    ef = info[:, :2].astype(jnp.int32).reshape(-1)
    onehot = (ef[:, None] == jnp.arange(N_EXPERTS, dtype=jnp.int32)[None, :]).astype(jnp.int32)
    rank = jnp.sum((jnp.cumsum(onehot, axis=0) - onehot) * onehot, axis=1)
    tiles_e = (jnp.sum(onehot, axis=0) + TMM - 1) // TMM
    tile_end = jnp.cumsum(tiles_e)
    n_used = tile_end[-1]
    dest = (tile_end - tiles_e)[ef] * TMM + rank
    n_tiles = (2 * nt) // TMM + N_EXPERTS
    tile_ids = jnp.minimum(jnp.arange(n_tiles, dtype=jnp.int32), n_used - 1)
    tile_expert = jnp.sum((tile_ids[:, None] >= tile_end[None, :]).astype(jnp.int32), axis=1)
    src = jnp.zeros((n_tiles * TMM,), jnp.int32).at[dest].set(jnp.arange(2 * nt, dtype=jnp.int32) // 2)
    return tile_expert, n_used.reshape(1), src.reshape(n_tiles, 1, TMM), dest.reshape(nt // TM, 1, 2 * TM)


def _row_gather(idx_ref, n_rows, src_hbm, dst_bufs, sem):
    nk = len(dst_bufs)

    def copy(r, k, row):
        return pltpu.make_async_copy(src_hbm.at[pl.ds(row, 1), :], dst_bufs[k].at[pl.ds(r, 1), :], sem)

    def start(r, carry):
        for k in range(nk):
            copy(r, k, idx_ref[0, 0, nk * r + k]).start()
        return carry

    def wait(r, carry):
        for k in range(nk):
            copy(r, k, 0).wait()
        return carry

    lax.fori_loop(0, n_rows, start, 0, unroll=8)
    lax.fori_loop(0, n_rows, wait, 0, unroll=8)


def _moe_ffn_kernel(te_ref, nu_ref, src_ref, h_hbm, w1_ref, w3_ref, w2_ref, o_ref,
                    xbuf, h_scr, acc_scr, sem):
    i = pl.program_id(0)
    f = pl.program_id(1)
    last_f = pl.num_programs(1) - 1
    used = i < nu_ref[0]

    @pl.when(jnp.logical_and(used, f == 0))
    def _():
        _row_gather(src_ref, TMM, h_hbm, (xbuf,), sem)
        h_scr[...] = xbuf[...].astype(BF16)
        acc_scr[...] = jnp.zeros_like(acc_scr)

    @pl.when(used)
    def _():
        h = h_scr[...]
        g = _silu(_dot(h, w1_ref[0])) * _dot(h, w3_ref[0])
        acc_scr[...] += _dot(g.astype(BF16), w2_ref[0])

    @pl.when(f == last_f)
    def _():
        o_ref[...] = jnp.where(used, acc_scr[...], 0.0)


def _moe_ffn(h, tile_expert, n_used, src, w1, w3, w2):
    n_tiles = src.shape[0]
    n_f = D_FF // TF
    fmap = lambda i, f, te, nu: jnp.where(i < nu[0], f, n_f - 1)
    return pl.pallas_call(
        _moe_ffn_kernel,
        out_shape=jax.ShapeDtypeStruct((n_tiles * TMM, D_MODEL), F32),
        grid_spec=pltpu.PrefetchScalarGridSpec(
            num_scalar_prefetch=2,
            grid=(n_tiles, n_f),
            in_specs=[pl.BlockSpec((1, 1, TMM), lambda i, f, te, nu: (i, 0, 0), memory_space=pltpu.SMEM),
                      pl.BlockSpec(memory_space=pl.ANY),
                      pl.BlockSpec((1, D_MODEL, TF), lambda i, f, te, nu: (te[i], 0, fmap(i, f, te, nu))),
                      pl.BlockSpec((1, D_MODEL, TF), lambda i, f, te, nu: (te[i], 0, fmap(i, f, te, nu))),
                      pl.BlockSpec((1, TF, D_MODEL), lambda i, f, te, nu: (te[i], fmap(i, f, te, nu), 0))],
            out_specs=pl.BlockSpec((TMM, D_MODEL), lambda i, f, te, nu: (i, 0)),
            scratch_shapes=[pltpu.VMEM((TMM, D_MODEL), F32), pltpu.VMEM((TMM, D_MODEL), BF16),
                            pltpu.VMEM((TMM, D_MODEL), F32), pltpu.SemaphoreType.DMA]),
        compiler_params=_cparams(("arbitrary", "arbitrary")),
        name="moe_ffn",
    )(tile_expert, n_used, src, h, w1, w3, w2)


def _moe_combine_kernel(dest_ref, x_ref, gate_ref, info_ref, ys_hbm, lng_ref, lnb_ref, o_ref,
                        buf0, buf1, sem, *, tiles_per_batch):
    _row_gather(dest_ref, TM, ys_hbm, (buf0, buf1), sem)
    info = info_ref[...]
    y = info[:, 2:3] * buf0[...] + info[:, 3:4] * buf1[...]
    gate = _mod_row(gate_ref, tiles_per_batch)
    o_ref[...] = _ln(DEEPNORM_ALPHA * x_ref[...] + gate * y) * lng_ref[...] + lnb_ref[...]


def _moe(x, mods_l, router, w1, w3, w2, lng, lnb, tiles_per_batch):
    nt = x.shape[0]
    h, info = _route(x, mods_l, router, tiles_per_batch)
    tile_expert, n_used, src, dest = _dispatch(info)
    ys = _moe_ffn(h, tile_expert, n_used, src, w1, w3, w2)
    vec = pl.BlockSpec((1, D_MODEL), lambda i: (0, 0))
    return pl.pallas_call(
        functools.partial(_moe_combine_kernel, tiles_per_batch=tiles_per_batch),
        out_shape=jax.ShapeDtypeStruct((nt, D_MODEL), F32),
        grid=(nt // TM,),
        in_specs=[pl.BlockSpec((1, 1, 2 * TM), lambda i: (i, 0, 0), memory_space=pltpu.SMEM),
                  pl.BlockSpec((TM, D_MODEL), lambda i: (i, 0)),
                  pl.BlockSpec((SUBLANES, D_MODEL), lambda i: (0, 5)),
                  pl.BlockSpec((TM, LANES), lambda i: (i, 0)),
                  pl.BlockSpec(memory_space=pl.ANY),
                  vec, vec],
        out_specs=pl.BlockSpec((TM, D_MODEL), lambda i: (i, 0)),
        scratch_shapes=[pltpu.VMEM((TM, D_MODEL), F32), pltpu.VMEM((TM, D_MODEL), F32),
                        pltpu.SemaphoreType.DMA],
        compiler_params=_cparams(("arbitrary",)),
        name="moe_combine",
    )(dest, x, mods_l, info, ys, lng, lnb)


def _rope_tables(l):
    n_freq = RET_HEAD_DIM // 4
    inv_freq = ROPE_BASE ** (-jnp.arange(n_freq, dtype=F32) / n_freq)
    pos = jnp.arange(l, dtype=jnp.int32)
    row = (pos // GRID_W).astype(F32)
    col = (pos % GRID_W).astype(F32)
    ang = jnp.concatenate([row[:, None] * inv_freq, col[:, None] * inv_freq], axis=-1)
    cos, sin = jnp.cos(ang), jnp.sin(ang)
    cos_t = jnp.concatenate([cos, cos], axis=-1)
    sin_t = jnp.concatenate([-sin, sin], axis=-1)
    cos_t = jnp.concatenate([cos_t, jnp.ones((PREP_ROWS, LANES), F32)], axis=0)
    sin_t = jnp.concatenate([sin_t, jnp.zeros((PREP_ROWS, LANES), F32)], axis=0)
    return cos_t, sin_t


def _split_w_in(w):
    o_ret = D_SSD + SSD_XBC + 2 * SSD_HEADS
    o_gdn = o_ret + 4 * D_RET
    z = w[:, 0:D_SSD]
    xbc = w[:, D_SSD:D_SSD + SSD_XBC]
    dt = w[:, D_SSD + SSD_XBC:o_ret]
    ret = w[:, o_ret:o_gdn]
    gqkv = w[:, o_gdn:o_gdn + 3 * D_GDN]
    gg = w[:, o_gdn + 3 * D_GDN:o_gdn + 4 * D_GDN]
    ab = w[:, o_gdn + 4 * D_GDN:]
    main = jnp.concatenate([xbc, gqkv, z, ret, gg], axis=1).astype(BF16)
    small = jnp.concatenate([dt, ab, jnp.zeros((w.shape[0], LANES - dt.shape[1] - ab.shape[1]), w.dtype)], axis=1)
    return main, small.astype(BF16), small.T.astype(BF16)


def _lane_params(ssd_vals, gdn_vals):
    row = jnp.zeros((LANES,), F32)
    row = row.at[SM_DT:SM_DT + 2 * SSD_HEADS].set(ssd_vals.reshape(-1))
    row = row.at[SM_A:SM_A + 2 * GDN_HEADS].set(gdn_vals.reshape(-1))
    return row


def kernel(x, c, ctx, c_ctx, w_mod, b_mod, w_in, ssd_conv_w, ssd_conv_b, ssd_a_log, ssd_dt_bias, ssd_d, ssd_norm_w, ret_log_decay, ret_norm_w, gdn_conv_w, gdn_a_log, gdn_dt_bias, gdn_norm_w, w_out, ln1_g, ln1_b, ln2_g, ln2_b, ffn_w1, ffn_w3, ffn_w2, moe_router, moe_w1, moe_w3, moe_w2):
    b, l, d = x.shape
    lc = ctx.shape[1]
    assert d == D_MODEL and l % TM == 0 and (b * lc) % TM == 0 and lc % PREP_ROWS == 0 and l % GRID_W == 0
    tiles_per_batch = l // TM

    xt = jnp.concatenate([x.reshape(b * l, d), ctx.reshape(b * lc, d)], axis=0)
    mods = _mods(c, c_ctx, w_mod, b_mod)
    cos_t, sin_t = _rope_tables(l)

    for i in range(DEPTH):
        w_main, w_small, w_small_t = _split_w_in(w_in[i])
        conv_w = jnp.concatenate([ssd_conv_w[i], gdn_conv_w[i]], axis=1)
        conv_b = jnp.concatenate([ssd_conv_b[i], jnp.zeros((3 * D_GDN,), F32)])[None, :]
        bias_row = _lane_params(ssd_dt_bias[i], gdn_dt_bias[i])
        alog_row = _lane_params(ssd_a_log[i], gdn_a_log[i])
        prow = jnp.zeros((SUBLANES, LANES), F32).at[0].set(bias_row).at[1].set(alog_row)
        pcol = prow.T
        ld = jnp.broadcast_to(ret_log_decay[i].reshape(2 * RET_HEADS, 1), (2 * RET_HEADS, LANES))

        p_main, p_small, p_small_t = _inproj(xt, mods[i], w_main, w_small, w_small_t, tiles_per_batch)
        prep = _prep(p_main, cos_t, sin_t, conv_w, conv_b, b, l, lc)
        ys = _ssd_scan(prep, p_small, p_small_t, prow, pcol, b, l, lc)
        yr = _ret_scan(prep, p_main, ld, b, l, lc)
        og = _gdn_scan(prep, p_small, p_small_t, prow, pcol, b, l, lc)
        xt = _outproj(xt, mods[i], ys, prep, p_main, yr, og,
                      jnp.repeat(ssd_d[i], SSD_HEAD_DIM)[None, :], ssd_norm_w[i][None, :],
                      ret_norm_w[i][None, :], gdn_norm_w[i][None, :], w_out[i].astype(BF16),
                      ln1_g[i][None, :], ln1_b[i][None, :], l // TM_OUT)
        j = i // 2
        if i % 2 == 0:
            xt = _ffn(xt, mods[i], ffn_w1[j].astype(BF16), ffn_w3[j].astype(BF16), ffn_w2[j].astype(BF16),
                      ln2_g[i][None, :], ln2_b[i][None, :], tiles_per_batch)
        else:
            router = jnp.concatenate([moe_router[j], jnp.zeros((d, LANES - N_EXPERTS), F32)], axis=1)
            xt = _moe(xt, mods[i], router, moe_w1[j].astype(BF16), moe_w3[j].astype(BF16),
                      moe_w2[j].astype(BF16), ln2_g[i][None, :], ln2_b[i][None, :], tiles_per_batch)
    return xt[:b * l].reshape(b, l, d)
```

```python
import functools

import jax
import jax.numpy as jnp
from jax import lax
from jax.experimental import pallas as pl
from jax.experimental.pallas import tpu as pltpu

F32 = jnp.float32
BF16 = jnp.bfloat16

D_MODEL = 2048
DEPTH = 4
GRID_W = 64
D_SSD = 1024
SSD_HEAD_DIM = 64
SSD_HEADS = 16
SSD_GROUPS = 2
SSD_STATE = 128
SSD_XBC = D_SSD + 2 * SSD_GROUPS * SSD_STATE
D_RET = 512
RET_HEAD_DIM = 128
RET_HEADS = 4
D_GDN = 512
GDN_HEAD_DIM = 128
GDN_HEADS = 4
CONV_W = 5
ROPE_BASE = 10000.0
D_FF = 5632
N_EXPERTS = 8
DEEPNORM_ALPHA = (2 * DEPTH) ** 0.25
EPS = 1e-6

LANES = 128
SUBLANES = 8
VMEM_LIMIT = 56 * 1024 * 1024

TM = 512
TM_OUT = 256
TMM = 512
PREP_ROWS = 256
CHUNK = 128
TN_IN = 1664
TF = 512
TN_MOD = 1024

COL_CONV = 0
COL_Z = 3072
COL_RET = 4096
COL_GG = 6144
N_MAIN = 6656
SM_DT = 0
SM_A = 32
SM_B = 40


def _silu(x):
    return x * jax.nn.sigmoid(x)


def _softplus(x):
    return jnp.maximum(x, 0.0) + jnp.log(1.0 + jnp.exp(-jnp.abs(x)))


def _ln(x):
    mu = jnp.mean(x, axis=-1, keepdims=True)
    xc = x - mu
    var = jnp.mean(xc * xc, axis=-1, keepdims=True)
    return xc * lax.rsqrt(var + EPS)


def _dot(a, b):
    return jnp.dot(a, b, preferred_element_type=F32)


def _dot_nt(a, b):
    return lax.dot_general(a, b, (((1,), (1,)), ((), ())), preferred_element_type=F32)


def _dot_tn(a, b):
    return lax.dot_general(a, b, (((0,), (0,)), ((), ())), preferred_element_type=F32)


def _split2(x):
    hi = x.astype(BF16)
    lo = (x - hi.astype(F32)).astype(BF16)
    return hi, lo


def _split3(x):
    hi = x.astype(BF16)
    r = x - hi.astype(F32)
    mid = r.astype(BF16)
    lo = (r - mid.astype(F32)).astype(BF16)
    return hi, mid, lo


def _dot_hi(a, b):
    ah, al = _split2(a)
    bh, bl = _split2(b)
    return _dot(ah, bh) + _dot(ah, bl) + _dot(al, bh)


def _dot01_lhs(m01, a):
    hi, mid, lo = _split3(a)
    return _dot(m01, hi) + _dot(m01, mid) + _dot(m01, lo)


def _dot01_rhs(a, m01):
    hi, mid, lo = _split3(a)
    return _dot(hi, m01) + _dot(mid, m01) + _dot(lo, m01)


def _cparams(sem):
    return pltpu.CompilerParams(dimension_semantics=sem, vmem_limit_bytes=VMEM_LIMIT)


def _mods_kernel(cb_ref, w_ref, b_ref, o_ref):
    tn = w_ref.shape[2]
    rows = []
    for r in range(3):
        act = _silu(cb_ref[r])
        pieces = [jnp.sum(w_ref[0, :, j * LANES:(j + 1) * LANES] * act, axis=0, keepdims=True)
                  for j in range(tn // LANES)]
        rows.append(jnp.concatenate(pieces, axis=1) + b_ref[0])
    rows.append(jnp.zeros((SUBLANES - 3, tn), F32))
    o_ref[0] = jnp.concatenate(rows, axis=0)


def _mods(c, c_ctx, w_mod, b_mod):
    cvec = jnp.concatenate([c, c_ctx[None, :]], axis=0)
    cb = jnp.broadcast_to(cvec[:, :, None], (3, D_MODEL, LANES))
    n = w_mod.shape[2]
    return pl.pallas_call(
        _mods_kernel,
        out_shape=jax.ShapeDtypeStruct((DEPTH, SUBLANES, n), F32),
        grid=(DEPTH, n // TN_MOD),
        in_specs=[pl.BlockSpec((3, D_MODEL, LANES), lambda l, j: (0, 0, 0)),
                  pl.BlockSpec((1, D_MODEL, TN_MOD), lambda l, j: (l, 0, j)),
                  pl.BlockSpec((1, 1, TN_MOD), lambda l, j: (l, 0, j))],
        out_specs=pl.BlockSpec((1, SUBLANES, TN_MOD), lambda l, j: (l, 0, j)),
        compiler_params=_cparams(("arbitrary", "arbitrary")),
        name="mods",
    )(cb, w_mod, b_mod.reshape(DEPTH, 1, n))


def _mod_row(ref, tiles_per_batch):
    s = jnp.minimum(pl.program_id(0) // tiles_per_batch, 2)
    return ref[pl.ds(s, 1), :]


def _inproj_kernel(x_ref, sh_ref, sc_ref, w_ref, ws_ref, wst_ref, p_ref, ps_ref, pst_ref, h_scr,
                   *, tiles_per_batch):
    @pl.when(pl.program_id(1) == 0)
    def _():
        shift = _mod_row(sh_ref, tiles_per_batch)
        scale = _mod_row(sc_ref, tiles_per_batch)
        hb = (_ln(x_ref[...]) * (1.0 + scale) + shift).astype(BF16)
        h_scr[...] = hb
        ps_ref[...] = _dot(hb, ws_ref[...])
        pst_ref[...] = _dot_nt(wst_ref[...], hb)

    p_ref[...] = _dot(h_scr[...], w_ref[...])


def _inproj(x, mods_l, w_main, w_small, w_small_t, tiles_per_batch):
    nt = x.shape[0]
    return pl.pallas_call(
        functools.partial(_inproj_kernel, tiles_per_batch=tiles_per_batch),
        out_shape=(jax.ShapeDtypeStruct((nt, N_MAIN), F32),
                   jax.ShapeDtypeStruct((nt, LANES), F32),
                   jax.ShapeDtypeStruct((LANES, nt), F32)),
        grid=(nt // TM, N_MAIN // TN_IN),
        in_specs=[pl.BlockSpec((TM, D_MODEL), lambda i, j: (i, 0)),
                  pl.BlockSpec((SUBLANES, D_MODEL), lambda i, j: (0, 0)),
                  pl.BlockSpec((SUBLANES, D_MODEL), lambda i, j: (0, 1)),
                  pl.BlockSpec((D_MODEL, TN_IN), lambda i, j: (0, j)),
                  pl.BlockSpec((D_MODEL, LANES), lambda i, j: (0, 0)),
                  pl.BlockSpec((LANES, D_MODEL), lambda i, j: (0, 0))],
        out_specs=(pl.BlockSpec((TM, TN_IN), lambda i, j: (i, j)),
                   pl.BlockSpec((TM, LANES), lambda i, j: (i, 0)),
                   pl.BlockSpec((LANES, TM), lambda i, j: (0, i))),
        scratch_shapes=[pltpu.VMEM((TM, D_MODEL), BF16)],
        compiler_params=_cparams(("arbitrary", "arbitrary")),
        name="inproj",
    )(x, mods_l, mods_l, w_main, w_small, w_small_t)


def _prep_kernel(main_ref, prev_ref, next_ref, rq_ref, rk_ref, cos_ref, sin_ref, cw_ref, cb_ref,
                 o_ref, ext_scr, *, seg_starts, seg_ends):
    i = pl.program_id(0)
    r = main_ref.shape[0]
    is_start = functools.reduce(jnp.logical_or, [i == s for s in seg_starts])
    is_end = functools.reduce(jnp.logical_or, [i == s for s in seg_ends])
    ext_scr[0:SUBLANES, :] = jnp.where(is_start, 0.0, prev_ref[...])
    ext_scr[SUBLANES:SUBLANES + r, :] = main_ref[...]
    ext_scr[SUBLANES + r:2 * SUBLANES + r, :] = jnp.where(is_end, 0.0, next_ref[...])

    half = CONV_W // 2
    n_conv = main_ref.shape[1]
    for c0 in range(0, n_conv, 512):
        cs = slice(c0, c0 + 512)
        acc = jnp.broadcast_to(cb_ref[:, cs], (r, 512))
        for j in range(CONV_W):
            acc = acc + ext_scr[SUBLANES - half + j:SUBLANES - half + j + r, cs] * cw_ref[j:j + 1, cs]
        act = _silu(acc)
        if c0 in (1536, 2048):
            scale = GDN_HEAD_DIM ** -0.5 if c0 == 1536 else 1.0
            for h in range(GDN_HEADS):
                xh = act[:, h * LANES:(h + 1) * LANES]
                inv = lax.rsqrt(jnp.sum(xh * xh, axis=-1, keepdims=True) + EPS)
                o_ref[:, c0 + h * LANES:c0 + (h + 1) * LANES] = xh * inv * scale
        else:
            o_ref[:, cs] = act

    cosf = cos_ref[...]
    sinf = sin_ref[...]
    for src, off, scale in ((rq_ref, 3072, 1.0), (rk_ref, 3584, RET_HEAD_DIM ** -0.5)):
        for h in range(RET_HEADS):
            xh = src[:, h * LANES:(h + 1) * LANES]
            rot = xh * cosf + pltpu.roll(xh, LANES // 2, axis=1) * sinf
            o_ref[:, off + h * LANES:off + (h + 1) * LANES] = rot * scale


def _prep(p_main, cos_t, sin_t, conv_w, conv_b, b, l, lc):
    nt = p_main.shape[0]
    r = PREP_ROWS
    n_chunks = nt // r
    lat_chunks = (b * l) // r
    seg_rows = [k * l for k in range(b)] + [b * l + k * lc for k in range(b)]
    seg_len = [l] * b + [lc] * b
    seg_starts = tuple(s // r for s in seg_rows)
    seg_ends = tuple((s + n) // r - 1 for s, n in zip(seg_rows, seg_len))
    sub_per = r // SUBLANES
    n_sub = nt // SUBLANES
    rope_map = lambda i: (jnp.where(i < lat_chunks, i % (l // r), l // r), 0)
    return pl.pallas_call(
        functools.partial(_prep_kernel, seg_starts=seg_starts, seg_ends=seg_ends),
        out_shape=jax.ShapeDtypeStruct((nt, 4096), F32),
        grid=(n_chunks,),
        in_specs=[pl.BlockSpec((r, 3072), lambda i: (i, 0)),
                  pl.BlockSpec((SUBLANES, 3072), lambda i: (jnp.maximum(i * sub_per - 1, 0), 0)),
                  pl.BlockSpec((SUBLANES, 3072), lambda i: (jnp.minimum((i + 1) * sub_per, n_sub - 1), 0)),
                  pl.BlockSpec((r, 512), lambda i: (i, COL_RET // 512)),
                  pl.BlockSpec((r, 512), lambda i: (i, COL_RET // 512 + 1)),
                  pl.BlockSpec((r, LANES), rope_map),
                  pl.BlockSpec((r, LANES), rope_map),
                  pl.BlockSpec((CONV_W, 3072), lambda i: (0, 0)),
                  pl.BlockSpec((1, 3072), lambda i: (0, 0))],
        out_specs=pl.BlockSpec((r, 4096), lambda i: (i, 0)),
        scratch_shapes=[pltpu.VMEM((r + 2 * SUBLANES, 3072), F32)],
        compiler_params=_cparams(("arbitrary",)),
        name="prep",
    )(p_main, p_main, p_main, p_main, p_main, cos_t, sin_t, conv_w, conv_b)


def _chunk_maps(b, l, lc):
    ncc, ncl = lc // CHUNK, l // CHUNK
    ctx0 = (b * l) // CHUNK

    def fwd(bi, s):
        return jnp.where(s < ncc, ctx0 + bi * ncc + s, bi * ncl + s - ncc)

    def bwd(bi, s):
        return jnp.where(s < ncc, ctx0 + bi * ncc + (ncc - 1 - s), bi * ncl + (ncl - 1 - (s - ncc)))

    return (fwd, bwd), ncc + ncl


def _tri_masks():
    tt = lax.broadcasted_iota(jnp.int32, (CHUNK, CHUNK), 0)
    ss = lax.broadcasted_iota(jnp.int32, (CHUNK, CHUNK), 1)
    return ss <= tt, ss >= tt


def _cumsums(la_c, la_r, d, low, upp):
    lowf = jnp.where(low, 1.0, 0.0).astype(BF16)
    uppf = jnp.where(upp, 1.0, 0.0).astype(BF16)
    if d == 0:
        return _dot01_lhs(lowf, la_c), _dot01_rhs(la_r, uppf), low
    return _dot01_lhs(uppf, la_c), _dot01_rhs(la_r, lowf), upp


def _ssd_kernel(xs_f, bc_f, sm_f, smt_f, xs_b, bc_b, sm_b, smt_b, prow_ref, pcol_ref,
                y_f, y_b, s_scr):
    @pl.when(pl.program_id(1) == 0)
    def _():
        s_scr[...] = jnp.zeros_like(s_scr)

    low, upp = _tri_masks()
    bias_r, alog_r = prow_ref[0:1, :], prow_ref[1:2, :]
    bias_c, alog_c = pcol_ref[:, 0:1], pcol_ref[:, 1:2]
    heads_per_group = SSD_HEADS // SSD_GROUPS
    group_w = heads_per_group * SSD_HEAD_DIM
    groups = []
    for d, (xs_ref, bc_ref, sm_ref, smt_ref, y_ref) in enumerate(
            ((xs_f, bc_f, sm_f, smt_f, y_f), (xs_b, bc_b, sm_b, smt_b, y_b))):
        dt_c = _softplus(sm_ref[...] + bias_r)
        dt_r = _softplus(smt_ref[...] + bias_c)
        g_c, g_r, mask = _cumsums(-dt_c * jnp.exp(alog_r), -dt_r * jnp.exp(alog_c), d, low, upp)
        last = CHUNK - 1 if d == 0 else 0
        for g in range(SSD_GROUPS):
            gs = slice(g * group_w, (g + 1) * group_w)
            kg = bc_ref[:, g * SSD_STATE:(g + 1) * SSD_STATE].astype(BF16)
            qg = bc_ref[:, (SSD_GROUPS + g) * SSD_STATE:(SSD_GROUPS + g + 1) * SSD_STATE].astype(BF16)
            qk = _dot_nt(qg, kg)
            heads = []
            for hh in range(heads_per_group):
                h = g * heads_per_group + hh
                ln = SM_DT + d * SSD_HEADS + h
                gcol = g_c[:, ln:ln + 1]
                grow = g_r[ln:ln + 1, :]
                glast = g_c[last:last + 1, ln:ln + 1]
                decay = jnp.exp(jnp.where(mask, gcol - grow, -jnp.inf))
                v = xs_ref[:, h * SSD_HEAD_DIM:(h + 1) * SSD_HEAD_DIM] * dt_c[:, ln:ln + 1]
                heads.append(dict(p=(qk * decay).astype(BF16), v16=v.astype(BF16), into=jnp.exp(gcol),
                                  vdec=v * jnp.exp(glast - gcol),
                                  carry=jnp.broadcast_to(jnp.exp(glast), (1, SSD_HEAD_DIM))))
            groups.append(dict(d=d, gs=gs, y_ref=y_ref, kg=kg, qg=qg, state=s_scr[d, :, gs], heads=heads))

    for w in groups:
        w["inter"] = _dot(w["qg"], w["state"].astype(BF16))
        for hd in w["heads"]:
            hd["intra"] = _dot(hd["p"], hd["v16"])
    for w in groups:
        y = [hd["intra"] + hd["into"] * w["inter"][:, hh * SSD_HEAD_DIM:(hh + 1) * SSD_HEAD_DIM]
             for hh, hd in enumerate(w["heads"])]
        w["y_ref"][:, w["gs"]] = jnp.concatenate(y, axis=1)
        vdec = jnp.concatenate([hd["vdec"] for hd in w["heads"]], axis=1).astype(BF16)
        carry = jnp.concatenate([hd["carry"] for hd in w["heads"]], axis=1)
        s_scr[w["d"], :, w["gs"]] = w["state"] * carry + _dot_tn(w["kg"], vdec)


def _ssd_scan(prep, p_small, p_small_t, prow, pcol, b, l, lc):
    nt = prep.shape[0]
    maps, steps = _chunk_maps(b, l, lc)
    in_specs = []
    for m in maps:
        in_specs += [pl.BlockSpec((CHUNK, D_SSD), lambda bi, s, m=m: (m(bi, s), 0)),
                     pl.BlockSpec((CHUNK, 512), lambda bi, s, m=m: (m(bi, s), 2)),
                     pl.BlockSpec((CHUNK, LANES), lambda bi, s, m=m: (m(bi, s), 0)),
                     pl.BlockSpec((LANES, CHUNK), lambda bi, s, m=m: (0, m(bi, s)))]
    in_specs += [pl.BlockSpec((SUBLANES, LANES), lambda bi, s: (0, 0)),
                 pl.BlockSpec((LANES, SUBLANES), lambda bi, s: (0, 0))]
    return pl.pallas_call(
        _ssd_kernel,
        out_shape=(jax.ShapeDtypeStruct((nt, D_SSD), F32),) * 2,
        grid=(b, steps),
        in_specs=in_specs,
        out_specs=tuple(pl.BlockSpec((CHUNK, D_SSD), lambda bi, s, m=m: (m(bi, s), 0)) for m in maps),
        scratch_shapes=[pltpu.VMEM((2, SSD_STATE, D_SSD), F32)],
        compiler_params=_cparams(("arbitrary", "arbitrary")),
        name="ssd_scan",
    )(prep, prep, p_small, p_small_t, prep, prep, p_small, p_small_t, prow, pcol)


def _ret_kernel(q_f, k_f, v_f, q_b, k_b, v_b, ld_ref, y_f, y_b, s_scr):
    @pl.when(pl.program_id(1) == 0)
    def _():
        s_scr[...] = jnp.zeros_like(s_scr)

    low, upp = _tri_masks()
    tt = lax.broadcasted_iota(jnp.int32, (CHUNK, CHUNK), 0)
    ss = lax.broadcasted_iota(jnp.int32, (CHUNK, CHUNK), 1)
    tcol = lax.broadcasted_iota(jnp.int32, (CHUNK, 1), 0)
    for d, (q_ref, k_ref, v_ref, y_ref) in enumerate(((q_f, k_f, v_f, y_f), (q_b, k_b, v_b, y_b))):
        mask = low if d == 0 else upp
        dist = (tt - ss if d == 0 else ss - tt).astype(F32)
        n_in = (tcol + 1 if d == 0 else CHUNK - tcol).astype(F32)
        n_out = (CHUNK - 1 - tcol if d == 0 else tcol).astype(F32)
        for h in range(RET_HEADS):
            hs = slice(h * RET_HEAD_DIM, (h + 1) * RET_HEAD_DIM)
            ld = ld_ref[d * RET_HEADS + h:d * RET_HEADS + h + 1, :]
            ld1 = ld[:, 0:1]
            decay = jnp.exp(jnp.where(mask, dist * ld, -jnp.inf))
            q = q_ref[:, hs].astype(BF16)
            k = k_ref[:, hs]
            v = v_ref[:, hs]
            state = s_scr[d, h]
            y_ref[:, hs] = (_dot((_dot_nt(q, k.astype(BF16)) * decay).astype(BF16), v.astype(BF16))
                            + jnp.exp(n_in * ld1) * _dot(q, state.astype(BF16)))
            kdec = (k * jnp.exp(n_out * ld1)).astype(BF16)
            s_scr[d, h] = state * jnp.exp(CHUNK * ld1) + _dot_tn(kdec, v.astype(BF16))


def _ret_scan(prep, p_main, ld, b, l, lc):
    nt = prep.shape[0]
    maps, steps = _chunk_maps(b, l, lc)
    in_specs = []
    for m in maps:
        in_specs += [pl.BlockSpec((CHUNK, D_RET), lambda bi, s, m=m: (m(bi, s), 6)),
                     pl.BlockSpec((CHUNK, D_RET), lambda bi, s, m=m: (m(bi, s), 7)),
                     pl.BlockSpec((CHUNK, D_RET), lambda bi, s, m=m: (m(bi, s), COL_RET // 512 + 2))]
    in_specs += [pl.BlockSpec((SUBLANES, LANES), lambda bi, s: (0, 0))]
    return pl.pallas_call(
        _ret_kernel,
        out_shape=(jax.ShapeDtypeStruct((nt, D_RET), F32),) * 2,
        grid=(b, steps),
        in_specs=in_specs,
        out_specs=tuple(pl.BlockSpec((CHUNK, D_RET), lambda bi, s, m=m: (m(bi, s), 0)) for m in maps),
        scratch_shapes=[pltpu.VMEM((2, RET_HEADS, RET_HEAD_DIM, RET_HEAD_DIM), F32)],
        compiler_params=_cparams(("arbitrary", "arbitrary")),
        name="ret_scan",
    )(prep, prep, p_main, prep, prep, p_main, ld)


def _gdn_kernel(q_f, k_f, v_f, sm_f, smt_f, q_b, k_b, v_b, sm_b, smt_b, prow_ref, pcol_ref,
                o_f, o_b, s_scr):
    @pl.when(pl.program_id(1) == 0)
    def _():
        s_scr[...] = jnp.zeros_like(s_scr)

    low, upp = _tri_masks()
    bias_r, alog_r = prow_ref[0:1, :], prow_ref[1:2, :]
    bias_c, alog_c = pcol_ref[:, 0:1], pcol_ref[:, 1:2]
    n_doublings = CHUNK.bit_length() - 1
    tt = lax.broadcasted_iota(jnp.int32, (CHUNK, CHUNK), 0)
    ss = lax.broadcasted_iota(jnp.int32, (CHUNK, CHUNK), 1)
    chains = []
    for d, (q_ref, k_ref, v_ref, sm_ref, smt_ref, o_ref) in enumerate(
            ((q_f, k_f, v_f, sm_f, smt_f, o_f), (q_b, k_b, v_b, sm_b, smt_b, o_b))):
        sm = sm_ref[...]
        la_c = -jnp.exp(alog_r) * _softplus(sm + bias_r)
        la_r = -jnp.exp(alog_c) * _softplus(smt_ref[...] + bias_c)
        beta_c = jax.nn.sigmoid(sm)
        g_c, g_r, mask = _cumsums(la_c, la_r, d, low, upp)
        strict = jnp.logical_and(low, jnp.logical_not(upp)) if d == 0 else jnp.logical_and(upp, jnp.logical_not(low))
        last = CHUNK - 1 if d == 0 else 0
        for h in range(GDN_HEADS):
            hs = slice(h * GDN_HEAD_DIM, (h + 1) * GDN_HEAD_DIM)
            la_ln = SM_A + d * GDN_HEADS + h
            b_ln = SM_B + d * GDN_HEADS + h
            gcol = g_c[:, la_ln:la_ln + 1]
            grow = g_r[la_ln:la_ln + 1, :]
            glast = g_c[last:last + 1, la_ln:la_ln + 1]
            beta = beta_c[:, b_ln:b_ln + 1]
            incl = jnp.exp(jnp.where(mask, gcol - grow, -jnp.inf))
            q = q_ref[:, hs]
            k = k_ref[:, hs]
            v = v_ref[:, hs]
            kb = k * beta
            k16 = k.astype(BF16)
            chains.append(dict(
                d=d, h=h, hs=hs, o_ref=o_ref,
                a_mat=jnp.where(strict, _dot_nt(kb.astype(BF16), k16) * incl, 0.0),
                rhs=jnp.concatenate([v * beta, kb * jnp.exp(gcol)], axis=1).astype(BF16),
                attn=(_dot_nt(q.astype(BF16), k16) * incl).astype(BF16),
                qdec=(q * jnp.exp(gcol)).astype(BF16),
                kdec=(k * jnp.exp(glast - gcol)).astype(BF16),
                carry=jnp.exp(glast)))

    eye = jnp.where(low & upp, 1.0, 0.0)
    for j in range(n_doublings):
        s = 1 << j
        same = (tt >> (j + 1)) == (ss >> (j + 1))
        t_hi, s_hi = (tt & s) != 0, (ss & s) != 0
        join = (same & t_hi & ~s_hi, same & ~t_hi & s_hi)
        if j == 0:
            for c in chains:
                c["inv"] = eye - jnp.where(join[c["d"]], c["a_mat"], 0.0)
            continue
        for c in chains:
            c["inv16"] = c["inv"].astype(BF16)
            c["tmp"] = _dot(jnp.where(join[c["d"]], c["a_mat"], 0.0).astype(BF16), c["inv16"]).astype(BF16)
        for c in chains:
            c["inv"] = c["inv"] - _dot(c["inv16"], c["tmp"])
    for c in chains:
        c["sol"] = _dot(c["inv"].astype(BF16), c["rhs"])
        c["state"] = s_scr[c["d"], c["h"]]
        c["s16"] = c["state"].astype(BF16)
    for c in chains:
        v_new = c["sol"][:, :GDN_HEAD_DIM] - _dot(c["sol"][:, GDN_HEAD_DIM:].astype(BF16), c["s16"])
        c["vn16"] = v_new.astype(BF16)
        c["qs"] = _dot(c["qdec"], c["s16"])
    for c in chains:
        c["o_ref"][:, c["hs"]] = c["qs"] + _dot(c["attn"], c["vn16"])
        s_scr[c["d"], c["h"]] = c["state"] * c["carry"] + _dot_tn(c["kdec"], c["vn16"])


def _gdn_scan(prep, p_small, p_small_t, prow, pcol, b, l, lc):
    nt = prep.shape[0]
    maps, steps = _chunk_maps(b, l, lc)
    in_specs = []
    for m in maps:
        in_specs += [pl.BlockSpec((CHUNK, D_GDN), lambda bi, s, m=m: (m(bi, s), 3)),
                     pl.BlockSpec((CHUNK, D_GDN), lambda bi, s, m=m: (m(bi, s), 4)),
                     pl.BlockSpec((CHUNK, D_GDN), lambda bi, s, m=m: (m(bi, s), 5)),
                     pl.BlockSpec((CHUNK, LANES), lambda bi, s, m=m: (m(bi, s), 0)),
                     pl.BlockSpec((LANES, CHUNK), lambda bi, s, m=m: (0, m(bi, s)))]
    in_specs += [pl.BlockSpec((SUBLANES, LANES), lambda bi, s: (0, 0)),
                 pl.BlockSpec((LANES, SUBLANES), lambda bi, s: (0, 0))]
    return pl.pallas_call(
        _gdn_kernel,
        out_shape=(jax.ShapeDtypeStruct((nt, D_GDN), F32),) * 2,
        grid=(b, steps),
        in_specs=in_specs,
        out_specs=tuple(pl.BlockSpec((CHUNK, D_GDN), lambda bi, s, m=m: (m(bi, s), 0)) for m in maps),
        scratch_shapes=[pltpu.VMEM((2, GDN_HEADS, GDN_HEAD_DIM, GDN_HEAD_DIM), F32)],
        compiler_params=_cparams(("arbitrary", "arbitrary")),
        name="gdn_scan",
    )(prep, prep, prep, p_small, p_small_t, prep, prep, prep, p_small, p_small_t, prow, pcol)


def _outproj_kernel(x_ref, gate_ref, ysf, ysb, xs_ref, z_ref, yrf, yrb, rg_ref, ogf, ogb, gg_ref,
                    dskip_ref, snw_ref, rnw_ref, gnw_ref, w_ref, lng_ref, lnb_ref, o_ref, cat_scr,
                    *, tiles_per_batch):
    y = ysf[...] + ysb[...] + dskip_ref[...] * xs_ref[...]
    y = y * _silu(z_ref[...])
    y = y * lax.rsqrt(jnp.mean(y * y, axis=-1, keepdims=True) + EPS)
    cat_scr[:, 0:D_SSD] = (y * snw_ref[...]).astype(BF16)
    for h in range(RET_HEADS):
        hs = slice(h * RET_HEAD_DIM, (h + 1) * RET_HEAD_DIM)
        yh = _ln(yrf[:, hs] + yrb[:, hs]) * rnw_ref[:, hs]
        cat_scr[:, D_SSD + h * RET_HEAD_DIM:D_SSD + (h + 1) * RET_HEAD_DIM] = (_silu(rg_ref[:, hs]) * yh).astype(BF16)
    off = D_SSD + D_RET
    for h in range(GDN_HEADS):
        hs = slice(h * GDN_HEAD_DIM, (h + 1) * GDN_HEAD_DIM)
        yh = ogf[:, hs] + ogb[:, hs]
        yh = yh * lax.rsqrt(jnp.mean(yh * yh, axis=-1, keepdims=True) + EPS) * gnw_ref[...]
        cat_scr[:, off + h * GDN_HEAD_DIM:off + (h + 1) * GDN_HEAD_DIM] = (_silu(gg_ref[:, hs]) * yh).astype(BF16)
    o = _dot(cat_scr[...], w_ref[...])
    gate = _mod_row(gate_ref, tiles_per_batch)
    o_ref[...] = _ln(DEEPNORM_ALPHA * x_ref[...] + gate * o) * lng_ref[...] + lnb_ref[...]


def _outproj(x, mods_l, ys, prep, p_main, yr, og, dskip, snw, rnw, gnw, w_out, lng, lnb, tiles_per_batch):
    nt = x.shape[0]
    row = lambda width, col: pl.BlockSpec((TM_OUT, width), lambda i: (i, col))
    vec = lambda width: pl.BlockSpec((1, width), lambda i: (0, 0))
    return pl.pallas_call(
        functools.partial(_outproj_kernel, tiles_per_batch=tiles_per_batch),
        out_shape=jax.ShapeDtypeStruct((nt, D_MODEL), F32),
        grid=(nt // TM_OUT,),
        in_specs=[row(D_MODEL, 0),
                  pl.BlockSpec((SUBLANES, D_MODEL), lambda i: (0, 2)),
                  row(D_SSD, 0), row(D_SSD, 0), row(D_SSD, 0), row(D_SSD, COL_Z // D_SSD),
                  row(D_RET, 0), row(D_RET, 0), row(D_RET, COL_RET // 512 + 3),
                  row(D_GDN, 0), row(D_GDN, 0), row(D_GDN, COL_GG // 512),
                  vec(D_SSD), vec(D_SSD), vec(D_RET), vec(GDN_HEAD_DIM),
                  pl.BlockSpec((D_MODEL, D_MODEL), lambda i: (0, 0)),
                  vec(D_MODEL), vec(D_MODEL)],
        out_specs=row(D_MODEL, 0),
        scratch_shapes=[pltpu.VMEM((TM_OUT, D_MODEL), BF16)],
        compiler_params=_cparams(("arbitrary",)),
        name="outproj",
    )(x, mods_l, ys[0], ys[1], prep, p_main, yr[0], yr[1], p_main, og[0], og[1], p_main,
      dskip, snw, rnw, gnw, w_out, lng, lnb)


def _ffn_kernel(x_ref, sh_ref, sc_ref, gate_ref, w1_ref, w3_ref, w2_ref, lng_ref, lnb_ref, o_ref,
                h_scr, acc_scr, *, tiles_per_batch):
    f = pl.program_id(1)

    @pl.when(f == 0)
    def _():
        shift = _mod_row(sh_ref, tiles_per_batch)
        scale = _mod_row(sc_ref, tiles_per_batch)
        h_scr[...] = (_ln(x_ref[...]) * (1.0 + scale) + shift).astype(BF16)
        acc_scr[...] = jnp.zeros_like(acc_scr)

    h = h_scr[...]
    g = _silu(_dot(h, w1_ref[...])) * _dot(h, w3_ref[...])
    acc_scr[...] += _dot(g.astype(BF16), w2_ref[...])

    @pl.when(f == pl.num_programs(1) - 1)
    def _():
        gate = _mod_row(gate_ref, tiles_per_batch)
        o_ref[...] = _ln(DEEPNORM_ALPHA * x_ref[...] + gate * acc_scr[...]) * lng_ref[...] + lnb_ref[...]


def _ffn(x, mods_l, w1, w3, w2, lng, lnb, tiles_per_batch):
    nt = x.shape[0]
    mod = lambda k: pl.BlockSpec((SUBLANES, D_MODEL), lambda i, f: (0, k))
    vec = pl.BlockSpec((1, D_MODEL), lambda i, f: (0, 0))
    return pl.pallas_call(
        functools.partial(_ffn_kernel, tiles_per_batch=tiles_per_batch),
        out_shape=jax.ShapeDtypeStruct((nt, D_MODEL), F32),
        grid=(nt // TM, D_FF // TF),
        in_specs=[pl.BlockSpec((TM, D_MODEL), lambda i, f: (i, 0)), mod(3), mod(4), mod(5),
                  pl.BlockSpec((D_MODEL, TF), lambda i, f: (0, f)),
                  pl.BlockSpec((D_MODEL, TF), lambda i, f: (0, f)),
                  pl.BlockSpec((TF, D_MODEL), lambda i, f: (f, 0)),
                  vec, vec],
        out_specs=pl.BlockSpec((TM, D_MODEL), lambda i, f: (i, 0)),
        scratch_shapes=[pltpu.VMEM((TM, D_MODEL), BF16), pltpu.VMEM((TM, D_MODEL), F32)],
        compiler_params=_cparams(("arbitrary", "arbitrary")),
        name="ffn",
    )(x, mods_l, mods_l, mods_l, w1, w3, w2, lng, lnb)


def _top2(logits):
    lane = lax.broadcasted_iota(jnp.int32, logits.shape, 1).astype(F32)
    neg = jnp.where(lane < N_EXPERTS, logits, -jnp.inf)
    m1 = jnp.max(neg, axis=-1, keepdims=True)
    i1 = jnp.min(jnp.where(neg == m1, lane, float(LANES)), axis=-1, keepdims=True)
    rest = jnp.where(lane == i1, -jnp.inf, neg)
    m2 = jnp.max(rest, axis=-1, keepdims=True)
    i2 = jnp.min(jnp.where(rest == m2, lane, float(LANES)), axis=-1, keepdims=True)
    e2 = jnp.exp(m2 - m1)
    inv = 1.0 / (1.0 + e2)
    return (jnp.where(lane == 0.0, i1, 0.0) + jnp.where(lane == 1.0, i2, 0.0)
            + jnp.where(lane == 2.0, inv, 0.0) + jnp.where(lane == 3.0, e2 * inv, 0.0))


def _route_kernel(x_ref, sh_ref, sc_ref, r_ref, h_ref, info_ref, *, tiles_per_batch):
    shift = _mod_row(sh_ref, tiles_per_batch)
    scale = _mod_row(sc_ref, tiles_per_batch)
    h = _ln(x_ref[...]) * (1.0 + scale) + shift
    h_ref[...] = h
    info_ref[...] = _top2(_dot_hi(h, r_ref[...]))


def _route(x, mods_l, router, tiles_per_batch):
    nt = x.shape[0]
    mod = lambda k: pl.BlockSpec((SUBLANES, D_MODEL), lambda i: (0, k))
    return pl.pallas_call(
        functools.partial(_route_kernel, tiles_per_batch=tiles_per_batch),
        out_shape=(jax.ShapeDtypeStruct((nt, D_MODEL), F32), jax.ShapeDtypeStruct((nt, LANES), F32)),
        grid=(nt // TM,),
        in_specs=[pl.BlockSpec((TM, D_MODEL), lambda i: (i, 0)), mod(3), mod(4),
                  pl.BlockSpec((D_MODEL, LANES), lambda i: (0, 0))],
        out_specs=(pl.BlockSpec((TM, D_MODEL), lambda i: (i, 0)), pl.BlockSpec((TM, LANES), lambda i: (i, 0))),
        compiler_params=_cparams(("arbitrary",)),
        name="route",
    )(x, mods_l, mods_l, router)


def _dispatch(info):
    nt = info.shape[0]
    ef = info[:, :2].astype(jnp.int32).reshape(-1)
    onehot = (ef[:, None] == jnp.arange(N_EXPERTS, dtype=jnp.int32)[None, :]).astype(jnp.int32)
    rank = jnp.sum((jnp.cumsum(onehot, axis=0) - onehot) * onehot, axis=1)
    tiles_e = (jnp.sum(onehot, axis=0) + TMM - 1) // TMM
    tile_end = jnp.cumsum(tiles_e)
    n_used = tile_end[-1]
    dest = (tile_end - tiles_e)[ef] * TMM + rank
    n_tiles = (2 * nt) // TMM + N_EXPERTS
    tile_ids = jnp.minimum(jnp.arange(n_tiles, dtype=jnp.int32), n_used - 1)
    tile_expert = jnp.sum((tile_ids[:, None] >= tile_end[None, :]).astype(jnp.int32), axis=1)
    src = jnp.zeros((n_tiles * TMM,), jnp.int32).at[dest].set(jnp.arange(2 * nt, dtype=jnp.int32) // 2)
    return tile_expert, n_used.reshape(1), src.reshape(n_tiles, 1, TMM), dest.reshape(nt // TM, 1, 2 * TM)


def _row_gather(idx_ref, n_rows, src_hbm, dst_bufs, sem):
    nk = len(dst_bufs)

    def copy(r, k, row):
        return pltpu.make_async_copy(src_hbm.at[pl.ds(row, 1), :], dst_bufs[k].at[pl.ds(r, 1), :], sem)

    def start(r, carry):
        for k in range(nk):
            copy(r, k, idx_ref[0, 0, nk * r + k]).start()
        return carry

    def wait(r, carry):
        for k in range(nk):
            copy(r, k, 0).wait()
        return carry

    lax.fori_loop(0, n_rows, start, 0, unroll=8)
    lax.fori_loop(0, n_rows, wait, 0, unroll=8)


def _moe_ffn_kernel(te_ref, nu_ref, src_ref, h_hbm, w1_ref, w3_ref, w2_ref, o_ref,
                    xbuf, h_scr, acc_scr, sem):
    i = pl.program_id(0)
    f = pl.program_id(1)
    last_f = pl.num_programs(1) - 1
    used = i < nu_ref[0]

    @pl.when(jnp.logical_and(used, f == 0))
    def _():
        _row_gather(src_ref, TMM, h_hbm, (xbuf,), sem)
        h_scr[...] = xbuf[...].astype(BF16)
        acc_scr[...] = jnp.zeros_like(acc_scr)

    @pl.when(used)
    def _():
        h = h_scr[...]
        g = _silu(_dot(h, w1_ref[0, 0])) * _dot(h, w3_ref[0, 0])
        acc_scr[...] += _dot(g.astype(BF16), w2_ref[0, 0])

    @pl.when(f == last_f)
    def _():
        o_ref[...] = jnp.where(used, acc_scr[...], 0.0)


def _moe_ffn(h, tile_expert, n_used, src, w1, w3, w2, layer):
    n_tiles = src.shape[0]
    n_f = D_FF // TF
    fmap = lambda i, f, te, nu: jnp.where(i < nu[0], f, n_f - 1)
    return pl.pallas_call(
        _moe_ffn_kernel,
        out_shape=jax.ShapeDtypeStruct((n_tiles * TMM, D_MODEL), F32),
        grid_spec=pltpu.PrefetchScalarGridSpec(
            num_scalar_prefetch=2,
            grid=(n_tiles, n_f),
            in_specs=[pl.BlockSpec((1, 1, TMM), lambda i, f, te, nu: (i, 0, 0), memory_space=pltpu.SMEM),
                      pl.BlockSpec(memory_space=pl.ANY),
                      pl.BlockSpec((1, 1, D_MODEL, TF), lambda i, f, te, nu: (layer, te[i], 0, fmap(i, f, te, nu))),
                      pl.BlockSpec((1, 1, D_MODEL, TF), lambda i, f, te, nu: (layer, te[i], 0, fmap(i, f, te, nu))),
                      pl.BlockSpec((1, 1, TF, D_MODEL), lambda i, f, te, nu: (layer, te[i], fmap(i, f, te, nu), 0))],
            out_specs=pl.BlockSpec((TMM, D_MODEL), lambda i, f, te, nu: (i, 0)),
            scratch_shapes=[pltpu.VMEM((TMM, D_MODEL), F32), pltpu.VMEM((TMM, D_MODEL), BF16),
                            pltpu.VMEM((TMM, D_MODEL), F32), pltpu.SemaphoreType.DMA]),
        compiler_params=_cparams(("arbitrary", "arbitrary")),
        name="moe_ffn",
    )(tile_expert, n_used, src, h, w1, w3, w2)


def _moe_combine_kernel(dest_ref, x_ref, gate_ref, info_ref, ys_hbm, lng_ref, lnb_ref, o_ref,
                        buf0, buf1, sem, *, tiles_per_batch):
    _row_gather(dest_ref, TM, ys_hbm, (buf0, buf1), sem)
    info = info_ref[...]
    y = info[:, 2:3] * buf0[...] + info[:, 3:4] * buf1[...]
    gate = _mod_row(gate_ref, tiles_per_batch)
    o_ref[...] = _ln(DEEPNORM_ALPHA * x_ref[...] + gate * y) * lng_ref[...] + lnb_ref[...]


def _moe(x, mods_l, router, w1, w3, w2, layer, lng, lnb, tiles_per_batch):
    nt = x.shape[0]
    h, info = _route(x, mods_l, router, tiles_per_batch)
    tile_expert, n_used, src, dest = _dispatch(info)
    ys = _moe_ffn(h, tile_expert, n_used, src, w1, w3, w2, layer)
    vec = pl.BlockSpec((1, D_MODEL), lambda i: (0, 0))
    return pl.pallas_call(
        functools.partial(_moe_combine_kernel, tiles_per_batch=tiles_per_batch),
        out_shape=jax.ShapeDtypeStruct((nt, D_MODEL), F32),
        grid=(nt // TM,),
        in_specs=[pl.BlockSpec((1, 1, 2 * TM), lambda i: (i, 0, 0), memory_space=pltpu.SMEM),
                  pl.BlockSpec((TM, D_MODEL), lambda i: (i, 0)),
                  pl.BlockSpec((SUBLANES, D_MODEL), lambda i: (0, 5)),
                  pl.BlockSpec((TM, LANES), lambda i: (i, 0)),
                  pl.BlockSpec(memory_space=pl.ANY),
                  vec, vec],
        out_specs=pl.BlockSpec((TM, D_MODEL), lambda i: (i, 0)),
        scratch_shapes=[pltpu.VMEM((TM, D_MODEL), F32), pltpu.VMEM((TM, D_MODEL), F32),
                        pltpu.SemaphoreType.DMA],
        compiler_params=_cparams(("arbitrary",)),
        name="moe_combine",
    )(dest, x, mods_l, info, ys, lng, lnb)


def _rope_tables(l):
    n_freq = RET_HEAD_DIM // 4
    inv_freq = ROPE_BASE ** (-jnp.arange(n_freq, dtype=F32) / n_freq)
    pos = jnp.arange(l, dtype=jnp.int32)
    row = (pos // GRID_W).astype(F32)
    col = (pos % GRID_W).astype(F32)
    ang = jnp.concatenate([row[:, None] * inv_freq, col[:, None] * inv_freq], axis=-1)
    cos, sin = jnp.cos(ang), jnp.sin(ang)
    cos_t = jnp.concatenate([cos, cos], axis=-1)
    sin_t = jnp.concatenate([-sin, sin], axis=-1)
    cos_t = jnp.concatenate([cos_t, jnp.ones((PREP_ROWS, LANES), F32)], axis=0)
    sin_t = jnp.concatenate([sin_t, jnp.zeros((PREP_ROWS, LANES), F32)], axis=0)
    return cos_t, sin_t


def _split_w_in(w):
    o_ret = D_SSD + SSD_XBC + 2 * SSD_HEADS
    o_gdn = o_ret + 4 * D_RET
    z = w[:, 0:D_SSD]
    xbc = w[:, D_SSD:D_SSD + SSD_XBC]
    dt = w[:, D_SSD + SSD_XBC:o_ret]
    ret = w[:, o_ret:o_gdn]
    gqkv = w[:, o_gdn:o_gdn + 3 * D_GDN]
    gg = w[:, o_gdn + 3 * D_GDN:o_gdn + 4 * D_GDN]
    ab = w[:, o_gdn + 4 * D_GDN:]
    main = jnp.concatenate([xbc, gqkv, z, ret, gg], axis=1).astype(BF16)
    small = jnp.concatenate([dt, ab, jnp.zeros((w.shape[0], LANES - dt.shape[1] - ab.shape[1]), w.dtype)], axis=1)
    return main, small.astype(BF16), small.T.astype(BF16)


def _lane_params(ssd_vals, gdn_vals):
    row = jnp.zeros((LANES,), F32)
    row = row.at[SM_DT:SM_DT + 2 * SSD_HEADS].set(ssd_vals.reshape(-1))
    row = row.at[SM_A:SM_A + 2 * GDN_HEADS].set(gdn_vals.reshape(-1))
    return row


def kernel(x, c, ctx, c_ctx, w_mod, b_mod, w_in, ssd_conv_w, ssd_conv_b, ssd_a_log, ssd_dt_bias, ssd_d, ssd_norm_w, ret_log_decay, ret_norm_w, gdn_conv_w, gdn_a_log, gdn_dt_bias, gdn_norm_w, w_out, ln1_g, ln1_b, ln2_g, ln2_b, ffn_w1, ffn_w3, ffn_w2, moe_router, moe_w1, moe_w3, moe_w2):
    b, l, d = x.shape
    lc = ctx.shape[1]
    assert d == D_MODEL and l % TM == 0 and (b * lc) % TM == 0 and lc % PREP_ROWS == 0 and l % GRID_W == 0
    tiles_per_batch = l // TM

    xt = jnp.concatenate([x.reshape(b * l, d), ctx.reshape(b * lc, d)], axis=0)
    mods = _mods(c, c_ctx, w_mod, b_mod)
    cos_t, sin_t = _rope_tables(l)
    moe_w1_16, moe_w3_16, moe_w2_16 = (w.astype(BF16) for w in (moe_w1, moe_w3, moe_w2))

    for i in range(DEPTH):
        w_main, w_small, w_small_t = _split_w_in(w_in[i])
        conv_w = jnp.concatenate([ssd_conv_w[i], gdn_conv_w[i]], axis=1)
        conv_b = jnp.concatenate([ssd_conv_b[i], jnp.zeros((3 * D_GDN,), F32)])[None, :]
        bias_row = _lane_params(ssd_dt_bias[i], gdn_dt_bias[i])
        alog_row = _lane_params(ssd_a_log[i], gdn_a_log[i])
        prow = jnp.zeros((SUBLANES, LANES), F32).at[0].set(bias_row).at[1].set(alog_row)
        pcol = prow.T
        ld = jnp.broadcast_to(ret_log_decay[i].reshape(2 * RET_HEADS, 1), (2 * RET_HEADS, LANES))

        p_main, p_small, p_small_t = _inproj(xt, mods[i], w_main, w_small, w_small_t, tiles_per_batch)
        prep = _prep(p_main, cos_t, sin_t, conv_w, conv_b, b, l, lc)
        ys = _ssd_scan(prep, p_small, p_small_t, prow, pcol, b, l, lc)
        yr = _ret_scan(prep, p_main, ld, b, l, lc)
        og = _gdn_scan(prep, p_small, p_small_t, prow, pcol, b, l, lc)
        xt = _outproj(xt, mods[i], ys, prep, p_main, yr, og,
                      jnp.repeat(ssd_d[i], SSD_HEAD_DIM)[None, :], ssd_norm_w[i][None, :],
                      ret_norm_w[i][None, :], gdn_norm_w[i][None, :], w_out[i].astype(BF16),
                      ln1_g[i][None, :], ln1_b[i][None, :], l // TM_OUT)
        j = i // 2
        if i % 2 == 0:
            xt = _ffn(xt, mods[i], ffn_w1[j].astype(BF16), ffn_w3[j].astype(BF16), ffn_w2[j].astype(BF16),
                      ln2_g[i][None, :], ln2_b[i][None, :], tiles_per_batch)
        else:
            router = jnp.concatenate([moe_router[j], jnp.zeros((d, LANES - N_EXPERTS), F32)], axis=1)
            xt = _moe(xt, mods[i], router, moe_w1_16, moe_w3_16, moe_w2_16, j,
                      ln2_g[i][None, :], ln2_b[i][None, :], tiles_per_batch)
    return xt[:b * l].reshape(b, l, d)
```

```python
import functools

import jax
import jax.numpy as jnp
from jax import lax
from jax.experimental import pallas as pl
from jax.experimental.pallas import tpu as pltpu

F32 = jnp.float32
BF16 = jnp.bfloat16

D_MODEL = 2048
DEPTH = 4
GRID_W = 64
D_SSD = 1024
SSD_HEAD_DIM = 64
SSD_HEADS = 16
SSD_GROUPS = 2
SSD_STATE = 128
SSD_XBC = D_SSD + 2 * SSD_GROUPS * SSD_STATE
D_RET = 512
RET_HEAD_DIM = 128
RET_HEADS = 4
D_GDN = 512
GDN_HEAD_DIM = 128
GDN_HEADS = 4
CONV_W = 5
ROPE_BASE = 10000.0
D_FF = 5632
N_EXPERTS = 8
DEEPNORM_ALPHA = (2 * DEPTH) ** 0.25
EPS = 1e-6

LANES = 128
SUBLANES = 8
VMEM_LIMIT = 56 * 1024 * 1024

TM = 512
TM_OUT = 256
TMM = 512
GATHER_UNROLL = 8
PREP_ROWS = 256
CHUNK = 128
TN_IN = 1664
TF = 512
TN_MOD = 1024

COL_CONV = 0
COL_Z = 3072
COL_RET = 4096
COL_GG = 6144
N_MAIN = 6656
SM_DT = 0
SM_A = 32
SM_B = 40


def _silu(x):
    return x * jax.nn.sigmoid(x)


def _softplus(x):
    return jnp.maximum(x, 0.0) + jnp.log(1.0 + jnp.exp(-jnp.abs(x)))


def _ln(x):
    mu = jnp.mean(x, axis=-1, keepdims=True)
    xc = x - mu
    var = jnp.mean(xc * xc, axis=-1, keepdims=True)
    return xc * lax.rsqrt(var + EPS)


def _dot(a, b):
    return jnp.dot(a, b, preferred_element_type=F32)


def _dot_nt(a, b):
    return lax.dot_general(a, b, (((1,), (1,)), ((), ())), preferred_element_type=F32)


def _dot_tn(a, b):
    return lax.dot_general(a, b, (((0,), (0,)), ((), ())), preferred_element_type=F32)


def _split2(x):
    hi = x.astype(BF16)
    lo = (x - hi.astype(F32)).astype(BF16)
    return hi, lo


def _split3(x):
    hi = x.astype(BF16)
    r = x - hi.astype(F32)
    mid = r.astype(BF16)
    lo = (r - mid.astype(F32)).astype(BF16)
    return hi, mid, lo


def _dot_hi(a, b):
    ah, al = _split2(a)
    bh, bl = _split2(b)
    return _dot(ah, bh) + _dot(ah, bl) + _dot(al, bh)


def _dot01_lhs(m01, a):
    hi, mid, lo = _split3(a)
    return _dot(m01, hi) + _dot(m01, mid) + _dot(m01, lo)


def _dot01_rhs(a, m01):
    hi, mid, lo = _split3(a)
    return _dot(hi, m01) + _dot(mid, m01) + _dot(lo, m01)


def _cparams(sem):
    return pltpu.CompilerParams(dimension_semantics=sem, vmem_limit_bytes=VMEM_LIMIT)


def _mods_kernel(cb_ref, w_ref, b_ref, o_ref):
    tn = w_ref.shape[2]
    rows = []
    for r in range(3):
        act = _silu(cb_ref[r])
        pieces = [jnp.sum(w_ref[0, :, j * LANES:(j + 1) * LANES] * act, axis=0, keepdims=True)
                  for j in range(tn // LANES)]
        rows.append(jnp.concatenate(pieces, axis=1) + b_ref[0])
    rows.append(jnp.zeros((SUBLANES - 3, tn), F32))
    o_ref[0] = jnp.concatenate(rows, axis=0)


def _mods(c, c_ctx, w_mod, b_mod):
    cvec = jnp.concatenate([c, c_ctx[None, :]], axis=0)
    cb = jnp.broadcast_to(cvec[:, :, None], (3, D_MODEL, LANES))
    n = w_mod.shape[2]
    return pl.pallas_call(
        _mods_kernel,
        out_shape=jax.ShapeDtypeStruct((DEPTH, SUBLANES, n), F32),
        grid=(DEPTH, n // TN_MOD),
        in_specs=[pl.BlockSpec((3, D_MODEL, LANES), lambda l, j: (0, 0, 0)),
                  pl.BlockSpec((1, D_MODEL, TN_MOD), lambda l, j: (l, 0, j)),
                  pl.BlockSpec((1, 1, TN_MOD), lambda l, j: (l, 0, j))],
        out_specs=pl.BlockSpec((1, SUBLANES, TN_MOD), lambda l, j: (l, 0, j)),
        compiler_params=_cparams(("arbitrary", "arbitrary")),
        name="mods",
    )(cb, w_mod, b_mod.reshape(DEPTH, 1, n))


def _mod_row(ref, tiles_per_batch):
    s = jnp.minimum(pl.program_id(0) // tiles_per_batch, 2)
    return ref[pl.ds(s, 1), :]


def _inproj_kernel(x_ref, sh_ref, sc_ref, w_ref, ws_ref, wst_ref, p_ref, ps_ref, pst_ref, h_scr,
                   *, tiles_per_batch):
    @pl.when(pl.program_id(1) == 0)
    def _():
        shift = _mod_row(sh_ref, tiles_per_batch)
        scale = _mod_row(sc_ref, tiles_per_batch)
        hb = (_ln(x_ref[...]) * (1.0 + scale) + shift).astype(BF16)
        h_scr[...] = hb
        ps_ref[...] = _dot(hb, ws_ref[...])
        pst_ref[...] = _dot_nt(wst_ref[...], hb)

    p_ref[...] = _dot(h_scr[...], w_ref[...])


def _inproj(x, mods_l, w_main, w_small, w_small_t, tiles_per_batch):
    nt = x.shape[0]
    return pl.pallas_call(
        functools.partial(_inproj_kernel, tiles_per_batch=tiles_per_batch),
        out_shape=(jax.ShapeDtypeStruct((nt, N_MAIN), F32),
                   jax.ShapeDtypeStruct((nt, LANES), F32),
                   jax.ShapeDtypeStruct((LANES, nt), F32)),
        grid=(nt // TM, N_MAIN // TN_IN),
        in_specs=[pl.BlockSpec((TM, D_MODEL), lambda i, j: (i, 0)),
                  pl.BlockSpec((SUBLANES, D_MODEL), lambda i, j: (0, 0)),
                  pl.BlockSpec((SUBLANES, D_MODEL), lambda i, j: (0, 1)),
                  pl.BlockSpec((D_MODEL, TN_IN), lambda i, j: (0, j)),
                  pl.BlockSpec((D_MODEL, LANES), lambda i, j: (0, 0)),
                  pl.BlockSpec((LANES, D_MODEL), lambda i, j: (0, 0))],
        out_specs=(pl.BlockSpec((TM, TN_IN), lambda i, j: (i, j)),
                   pl.BlockSpec((TM, LANES), lambda i, j: (i, 0)),
                   pl.BlockSpec((LANES, TM), lambda i, j: (0, i))),
        scratch_shapes=[pltpu.VMEM((TM, D_MODEL), BF16)],
        compiler_params=_cparams(("arbitrary", "arbitrary")),
        name="inproj",
    )(x, mods_l, mods_l, w_main, w_small, w_small_t)


def _prep_kernel(main_ref, prev_ref, next_ref, rq_ref, rk_ref, cos_ref, sin_ref, cw_ref, cb_ref,
                 o_ref, ext_scr, *, seg_starts, seg_ends):
    i = pl.program_id(0)
    r = main_ref.shape[0]
    is_start = functools.reduce(jnp.logical_or, [i == s for s in seg_starts])
    is_end = functools.reduce(jnp.logical_or, [i == s for s in seg_ends])
    ext_scr[0:SUBLANES, :] = jnp.where(is_start, 0.0, prev_ref[...])
    ext_scr[SUBLANES:SUBLANES + r, :] = main_ref[...]
    ext_scr[SUBLANES + r:2 * SUBLANES + r, :] = jnp.where(is_end, 0.0, next_ref[...])

    half = CONV_W // 2
    n_conv = main_ref.shape[1]
    for c0 in range(0, n_conv, 512):
        cs = slice(c0, c0 + 512)
        acc = jnp.broadcast_to(cb_ref[:, cs], (r, 512))
        for j in range(CONV_W):
            acc = acc + ext_scr[SUBLANES - half + j:SUBLANES - half + j + r, cs] * cw_ref[j:j + 1, cs]
        act = _silu(acc)
        if c0 in (1536, 2048):
            scale = GDN_HEAD_DIM ** -0.5 if c0 == 1536 else 1.0
            for h in range(GDN_HEADS):
                xh = act[:, h * LANES:(h + 1) * LANES]
                inv = lax.rsqrt(jnp.sum(xh * xh, axis=-1, keepdims=True) + EPS)
                o_ref[:, c0 + h * LANES:c0 + (h + 1) * LANES] = xh * inv * scale
        else:
            o_ref[:, cs] = act

    cosf = cos_ref[...]
    sinf = sin_ref[...]
    for src, off, scale in ((rq_ref, 3072, 1.0), (rk_ref, 3584, RET_HEAD_DIM ** -0.5)):
        for h in range(RET_HEADS):
            xh = src[:, h * LANES:(h + 1) * LANES]
            rot = xh * cosf + pltpu.roll(xh, LANES // 2, axis=1) * sinf
            o_ref[:, off + h * LANES:off + (h + 1) * LANES] = rot * scale


def _prep(p_main, cos_t, sin_t, conv_w, conv_b, b, l, lc):
    nt = p_main.shape[0]
    r = PREP_ROWS
    n_chunks = nt // r
    lat_chunks = (b * l) // r
    seg_rows = [k * l for k in range(b)] + [b * l + k * lc for k in range(b)]
    seg_len = [l] * b + [lc] * b
    seg_starts = tuple(s // r for s in seg_rows)
    seg_ends = tuple((s + n) // r - 1 for s, n in zip(seg_rows, seg_len))
    sub_per = r // SUBLANES
    n_sub = nt // SUBLANES
    rope_map = lambda i: (jnp.where(i < lat_chunks, i % (l // r), l // r), 0)
    return pl.pallas_call(
        functools.partial(_prep_kernel, seg_starts=seg_starts, seg_ends=seg_ends),
        out_shape=jax.ShapeDtypeStruct((nt, 4096), F32),
        grid=(n_chunks,),
        in_specs=[pl.BlockSpec((r, 3072), lambda i: (i, 0)),
                  pl.BlockSpec((SUBLANES, 3072), lambda i: (jnp.maximum(i * sub_per - 1, 0), 0)),
                  pl.BlockSpec((SUBLANES, 3072), lambda i: (jnp.minimum((i + 1) * sub_per, n_sub - 1), 0)),
                  pl.BlockSpec((r, 512), lambda i: (i, COL_RET // 512)),
                  pl.BlockSpec((r, 512), lambda i: (i, COL_RET // 512 + 1)),
                  pl.BlockSpec((r, LANES), rope_map),
                  pl.BlockSpec((r, LANES), rope_map),
                  pl.BlockSpec((CONV_W, 3072), lambda i: (0, 0)),
                  pl.BlockSpec((1, 3072), lambda i: (0, 0))],
        out_specs=pl.BlockSpec((r, 4096), lambda i: (i, 0)),
        scratch_shapes=[pltpu.VMEM((r + 2 * SUBLANES, 3072), F32)],
        compiler_params=_cparams(("arbitrary",)),
        name="prep",
    )(p_main, p_main, p_main, p_main, p_main, cos_t, sin_t, conv_w, conv_b)


def _chunk_maps(b, l, lc):
    ncc, ncl = lc // CHUNK, l // CHUNK
    ctx0 = (b * l) // CHUNK

    def fwd(bi, s):
        return jnp.where(s < ncc, ctx0 + bi * ncc + s, bi * ncl + s - ncc)

    def bwd(bi, s):
        return jnp.where(s < ncc, ctx0 + bi * ncc + (ncc - 1 - s), bi * ncl + (ncl - 1 - (s - ncc)))

    return (fwd, bwd), ncc + ncl


def _tri_masks():
    tt = lax.broadcasted_iota(jnp.int32, (CHUNK, CHUNK), 0)
    ss = lax.broadcasted_iota(jnp.int32, (CHUNK, CHUNK), 1)
    return ss <= tt, ss >= tt


def _cumsums(la_c, la_r, d, low, upp):
    lowf = jnp.where(low, 1.0, 0.0).astype(BF16)
    uppf = jnp.where(upp, 1.0, 0.0).astype(BF16)
    if d == 0:
        return _dot01_lhs(lowf, la_c), _dot01_rhs(la_r, uppf), low
    return _dot01_lhs(uppf, la_c), _dot01_rhs(la_r, lowf), upp


def _ssd_kernel(xs_f, bc_f, sm_f, smt_f, xs_b, bc_b, sm_b, smt_b, prow_ref, pcol_ref,
                y_f, y_b, s_scr):
    @pl.when(pl.program_id(1) == 0)
    def _():
        s_scr[...] = jnp.zeros_like(s_scr)

    low, upp = _tri_masks()
    bias_r, alog_r = prow_ref[0:1, :], prow_ref[1:2, :]
    bias_c, alog_c = pcol_ref[:, 0:1], pcol_ref[:, 1:2]
    heads_per_group = SSD_HEADS // SSD_GROUPS
    group_w = heads_per_group * SSD_HEAD_DIM
    groups = []
    for d, (xs_ref, bc_ref, sm_ref, smt_ref, y_ref) in enumerate(
            ((xs_f, bc_f, sm_f, smt_f, y_f), (xs_b, bc_b, sm_b, smt_b, y_b))):
        dt_c = _softplus(sm_ref[...] + bias_r)
        dt_r = _softplus(smt_ref[...] + bias_c)
        g_c, g_r, mask = _cumsums(-dt_c * jnp.exp(alog_r), -dt_r * jnp.exp(alog_c), d, low, upp)
        last = CHUNK - 1 if d == 0 else 0
        for g in range(SSD_GROUPS):
            gs = slice(g * group_w, (g + 1) * group_w)
            kg = bc_ref[:, g * SSD_STATE:(g + 1) * SSD_STATE].astype(BF16)
            qg = bc_ref[:, (SSD_GROUPS + g) * SSD_STATE:(SSD_GROUPS + g + 1) * SSD_STATE].astype(BF16)
            qk = _dot_nt(qg, kg)
            heads = []
            for hh in range(heads_per_group):
                h = g * heads_per_group + hh
                ln = SM_DT + d * SSD_HEADS + h
                gcol = g_c[:, ln:ln + 1]
                grow = g_r[ln:ln + 1, :]
                glast = g_c[last:last + 1, ln:ln + 1]
                decay = jnp.exp(jnp.where(mask, gcol - grow, -jnp.inf))
                v = xs_ref[:, h * SSD_HEAD_DIM:(h + 1) * SSD_HEAD_DIM] * dt_c[:, ln:ln + 1]
                heads.append(dict(p=(qk * decay).astype(BF16), v16=v.astype(BF16), into=jnp.exp(gcol),
                                  vdec=v * jnp.exp(glast - gcol),
                                  carry=jnp.broadcast_to(jnp.exp(glast), (1, SSD_HEAD_DIM))))
            groups.append(dict(d=d, gs=gs, y_ref=y_ref, kg=kg, qg=qg, state=s_scr[d, :, gs], heads=heads))

    for w in groups:
        w["inter"] = _dot(w["qg"], w["state"].astype(BF16))
        for hd in w["heads"]:
            hd["intra"] = _dot(hd["p"], hd["v16"])
    for w in groups:
        y = [hd["intra"] + hd["into"] * w["inter"][:, hh * SSD_HEAD_DIM:(hh + 1) * SSD_HEAD_DIM]
             for hh, hd in enumerate(w["heads"])]
        w["y_ref"][:, w["gs"]] = jnp.concatenate(y, axis=1)
        vdec = jnp.concatenate([hd["vdec"] for hd in w["heads"]], axis=1).astype(BF16)
        carry = jnp.concatenate([hd["carry"] for hd in w["heads"]], axis=1)
        s_scr[w["d"], :, w["gs"]] = w["state"] * carry + _dot_tn(w["kg"], vdec)


def _ssd_scan(prep, p_small, p_small_t, prow, pcol, b, l, lc):
    nt = prep.shape[0]
    maps, steps = _chunk_maps(b, l, lc)
    in_specs = []
    for m in maps:
        in_specs += [pl.BlockSpec((CHUNK, D_SSD), lambda bi, s, m=m: (m(bi, s), 0)),
                     pl.BlockSpec((CHUNK, 512), lambda bi, s, m=m: (m(bi, s), 2)),
                     pl.BlockSpec((CHUNK, LANES), lambda bi, s, m=m: (m(bi, s), 0)),
                     pl.BlockSpec((LANES, CHUNK), lambda bi, s, m=m: (0, m(bi, s)))]
    in_specs += [pl.BlockSpec((SUBLANES, LANES), lambda bi, s: (0, 0)),
                 pl.BlockSpec((LANES, SUBLANES), lambda bi, s: (0, 0))]
    return pl.pallas_call(
        _ssd_kernel,
        out_shape=(jax.ShapeDtypeStruct((nt, D_SSD), F32),) * 2,
        grid=(b, steps),
        in_specs=in_specs,
        out_specs=tuple(pl.BlockSpec((CHUNK, D_SSD), lambda bi, s, m=m: (m(bi, s), 0)) for m in maps),
        scratch_shapes=[pltpu.VMEM((2, SSD_STATE, D_SSD), F32)],
        compiler_params=_cparams(("arbitrary", "arbitrary")),
        name="ssd_scan",
    )(prep, prep, p_small, p_small_t, prep, prep, p_small, p_small_t, prow, pcol)


def _ret_kernel(q_f, k_f, v_f, q_b, k_b, v_b, ld_ref, y_f, y_b, s_scr):
    @pl.when(pl.program_id(1) == 0)
    def _():
        s_scr[...] = jnp.zeros_like(s_scr)

    low, upp = _tri_masks()
    tt = lax.broadcasted_iota(jnp.int32, (CHUNK, CHUNK), 0)
    ss = lax.broadcasted_iota(jnp.int32, (CHUNK, CHUNK), 1)
    tcol = lax.broadcasted_iota(jnp.int32, (CHUNK, 1), 0)
    for d, (q_ref, k_ref, v_ref, y_ref) in enumerate(((q_f, k_f, v_f, y_f), (q_b, k_b, v_b, y_b))):
        mask = low if d == 0 else upp
        dist = (tt - ss if d == 0 else ss - tt).astype(F32)
        n_in = (tcol + 1 if d == 0 else CHUNK - tcol).astype(F32)
        n_out = (CHUNK - 1 - tcol if d == 0 else tcol).astype(F32)
        for h in range(RET_HEADS):
            hs = slice(h * RET_HEAD_DIM, (h + 1) * RET_HEAD_DIM)
            ld = ld_ref[d * RET_HEADS + h:d * RET_HEADS + h + 1, :]
            ld1 = ld[:, 0:1]
            decay = jnp.exp(jnp.where(mask, dist * ld, -jnp.inf))
            q = q_ref[:, hs].astype(BF16)
            k = k_ref[:, hs]
            v = v_ref[:, hs]
            state = s_scr[d, h]
            y_ref[:, hs] = (_dot((_dot_nt(q, k.astype(BF16)) * decay).astype(BF16), v.astype(BF16))
                            + jnp.exp(n_in * ld1) * _dot(q, state.astype(BF16)))
            kdec = (k * jnp.exp(n_out * ld1)).astype(BF16)
            s_scr[d, h] = state * jnp.exp(CHUNK * ld1) + _dot_tn(kdec, v.astype(BF16))


def _ret_scan(prep, p_main, ld, b, l, lc):
    nt = prep.shape[0]
    maps, steps = _chunk_maps(b, l, lc)
    in_specs = []
    for m in maps:
        in_specs += [pl.BlockSpec((CHUNK, D_RET), lambda bi, s, m=m: (m(bi, s), 6)),
                     pl.BlockSpec((CHUNK, D_RET), lambda bi, s, m=m: (m(bi, s), 7)),
                     pl.BlockSpec((CHUNK, D_RET), lambda bi, s, m=m: (m(bi, s), COL_RET // 512 + 2))]
    in_specs += [pl.BlockSpec((SUBLANES, LANES), lambda bi, s: (0, 0))]
    return pl.pallas_call(
        _ret_kernel,
        out_shape=(jax.ShapeDtypeStruct((nt, D_RET), F32),) * 2,
        grid=(b, steps),
        in_specs=in_specs,
        out_specs=tuple(pl.BlockSpec((CHUNK, D_RET), lambda bi, s, m=m: (m(bi, s), 0)) for m in maps),
        scratch_shapes=[pltpu.VMEM((2, RET_HEADS, RET_HEAD_DIM, RET_HEAD_DIM), F32)],
        compiler_params=_cparams(("arbitrary", "arbitrary")),
        name="ret_scan",
    )(prep, prep, p_main, prep, prep, p_main, ld)


def _gdn_kernel(q_f, k_f, v_f, sm_f, smt_f, q_b, k_b, v_b, sm_b, smt_b, prow_ref, pcol_ref,
                o_f, o_b, s_scr):
    @pl.when(pl.program_id(1) == 0)
    def _():
        s_scr[...] = jnp.zeros_like(s_scr)

    low, upp = _tri_masks()
    bias_r, alog_r = prow_ref[0:1, :], prow_ref[1:2, :]
    bias_c, alog_c = pcol_ref[:, 0:1], pcol_ref[:, 1:2]
    n_doublings = CHUNK.bit_length() - 1
    tt = lax.broadcasted_iota(jnp.int32, (CHUNK, CHUNK), 0)
    ss = lax.broadcasted_iota(jnp.int32, (CHUNK, CHUNK), 1)
    chains = []
    for d, (q_ref, k_ref, v_ref, sm_ref, smt_ref, o_ref) in enumerate(
            ((q_f, k_f, v_f, sm_f, smt_f, o_f), (q_b, k_b, v_b, sm_b, smt_b, o_b))):
        sm = sm_ref[...]
        la_c = -jnp.exp(alog_r) * _softplus(sm + bias_r)
        la_r = -jnp.exp(alog_c) * _softplus(smt_ref[...] + bias_c)
        beta_c = jax.nn.sigmoid(sm)
        g_c, g_r, mask = _cumsums(la_c, la_r, d, low, upp)
        strict = jnp.logical_and(low, jnp.logical_not(upp)) if d == 0 else jnp.logical_and(upp, jnp.logical_not(low))
        last = CHUNK - 1 if d == 0 else 0
        for h in range(GDN_HEADS):
            hs = slice(h * GDN_HEAD_DIM, (h + 1) * GDN_HEAD_DIM)
            la_ln = SM_A + d * GDN_HEADS + h
            b_ln = SM_B + d * GDN_HEADS + h
            gcol = g_c[:, la_ln:la_ln + 1]
            grow = g_r[la_ln:la_ln + 1, :]
            glast = g_c[last:last + 1, la_ln:la_ln + 1]
            beta = beta_c[:, b_ln:b_ln + 1]
            incl = jnp.exp(jnp.where(mask, gcol - grow, -jnp.inf))
            q = q_ref[:, hs]
            k = k_ref[:, hs]
            v = v_ref[:, hs]
            kb = k * beta
            k16 = k.astype(BF16)
            chains.append(dict(
                d=d, h=h, hs=hs, o_ref=o_ref,
                a_mat=jnp.where(strict, _dot_nt(kb.astype(BF16), k16) * incl, 0.0),
                rhs=jnp.concatenate([v * beta, kb * jnp.exp(gcol)], axis=1).astype(BF16),
                attn=(_dot_nt(q.astype(BF16), k16) * incl).astype(BF16),
                qdec=(q * jnp.exp(gcol)).astype(BF16),
                kdec=(k * jnp.exp(glast - gcol)).astype(BF16),
                carry=jnp.exp(glast)))

    eye = jnp.where(low & upp, 1.0, 0.0)
    for j in range(n_doublings):
        s = 1 << j
        same = (tt >> (j + 1)) == (ss >> (j + 1))
        t_hi, s_hi = (tt & s) != 0, (ss & s) != 0
        join = (same & t_hi & ~s_hi, same & ~t_hi & s_hi)
        if j == 0:
            for c in chains:
                c["inv"] = eye - jnp.where(join[c["d"]], c["a_mat"], 0.0)
            continue
        for c in chains:
            c["inv16"] = c["inv"].astype(BF16)
            c["tmp"] = _dot(jnp.where(join[c["d"]], c["a_mat"], 0.0).astype(BF16), c["inv16"]).astype(BF16)
        for c in chains:
            c["inv"] = c["inv"] - _dot(c["inv16"], c["tmp"])
    for c in chains:
        c["sol"] = _dot(c["inv"].astype(BF16), c["rhs"])
        c["state"] = s_scr[c["d"], c["h"]]
        c["s16"] = c["state"].astype(BF16)
    for c in chains:
        v_new = c["sol"][:, :GDN_HEAD_DIM] - _dot(c["sol"][:, GDN_HEAD_DIM:].astype(BF16), c["s16"])
        c["vn16"] = v_new.astype(BF16)
        c["qs"] = _dot(c["qdec"], c["s16"])
    for c in chains:
        c["o_ref"][:, c["hs"]] = c["qs"] + _dot(c["attn"], c["vn16"])
        s_scr[c["d"], c["h"]] = c["state"] * c["carry"] + _dot_tn(c["kdec"], c["vn16"])


def _gdn_scan(prep, p_small, p_small_t, prow, pcol, b, l, lc):
    nt = prep.shape[0]
    maps, steps = _chunk_maps(b, l, lc)
    in_specs = []
    for m in maps:
        in_specs += [pl.BlockSpec((CHUNK, D_GDN), lambda bi, s, m=m: (m(bi, s), 3)),
                     pl.BlockSpec((CHUNK, D_GDN), lambda bi, s, m=m: (m(bi, s), 4)),
                     pl.BlockSpec((CHUNK, D_GDN), lambda bi, s, m=m: (m(bi, s), 5)),
                     pl.BlockSpec((CHUNK, LANES), lambda bi, s, m=m: (m(bi, s), 0)),
                     pl.BlockSpec((LANES, CHUNK), lambda bi, s, m=m: (0, m(bi, s)))]
    in_specs += [pl.BlockSpec((SUBLANES, LANES), lambda bi, s: (0, 0)),
                 pl.BlockSpec((LANES, SUBLANES), lambda bi, s: (0, 0))]
    return pl.pallas_call(
        _gdn_kernel,
        out_shape=(jax.ShapeDtypeStruct((nt, D_GDN), F32),) * 2,
        grid=(b, steps),
        in_specs=in_specs,
        out_specs=tuple(pl.BlockSpec((CHUNK, D_GDN), lambda bi, s, m=m: (m(bi, s), 0)) for m in maps),
        scratch_shapes=[pltpu.VMEM((2, GDN_HEADS, GDN_HEAD_DIM, GDN_HEAD_DIM), F32)],
        compiler_params=_cparams(("arbitrary", "arbitrary")),
        name="gdn_scan",
    )(prep, prep, prep, p_small, p_small_t, prep, prep, prep, p_small, p_small_t, prow, pcol)


def _outproj_kernel(x_ref, gate_ref, ysf, ysb, xs_ref, z_ref, yrf, yrb, rg_ref, ogf, ogb, gg_ref,
                    dskip_ref, snw_ref, rnw_ref, gnw_ref, w_ref, lng_ref, lnb_ref, o_ref, cat_scr,
                    *, tiles_per_batch):
    y = ysf[...] + ysb[...] + dskip_ref[...] * xs_ref[...]
    y = y * _silu(z_ref[...])
    y = y * lax.rsqrt(jnp.mean(y * y, axis=-1, keepdims=True) + EPS)
    cat_scr[:, 0:D_SSD] = (y * snw_ref[...]).astype(BF16)
    for h in range(RET_HEADS):
        hs = slice(h * RET_HEAD_DIM, (h + 1) * RET_HEAD_DIM)
        yh = _ln(yrf[:, hs] + yrb[:, hs]) * rnw_ref[:, hs]
        cat_scr[:, D_SSD + h * RET_HEAD_DIM:D_SSD + (h + 1) * RET_HEAD_DIM] = (_silu(rg_ref[:, hs]) * yh).astype(BF16)
    off = D_SSD + D_RET
    for h in range(GDN_HEADS):
        hs = slice(h * GDN_HEAD_DIM, (h + 1) * GDN_HEAD_DIM)
        yh = ogf[:, hs] + ogb[:, hs]
        yh = yh * lax.rsqrt(jnp.mean(yh * yh, axis=-1, keepdims=True) + EPS) * gnw_ref[...]
        cat_scr[:, off + h * GDN_HEAD_DIM:off + (h + 1) * GDN_HEAD_DIM] = (_silu(gg_ref[:, hs]) * yh).astype(BF16)
    o = _dot(cat_scr[...], w_ref[...])
    gate = _mod_row(gate_ref, tiles_per_batch)
    o_ref[...] = _ln(DEEPNORM_ALPHA * x_ref[...] + gate * o) * lng_ref[...] + lnb_ref[...]


def _outproj(x, mods_l, ys, prep, p_main, yr, og, dskip, snw, rnw, gnw, w_out, lng, lnb, tiles_per_batch):
    nt = x.shape[0]
    row = lambda width, col: pl.BlockSpec((TM_OUT, width), lambda i: (i, col))
    vec = lambda width: pl.BlockSpec((1, width), lambda i: (0, 0))
    return pl.pallas_call(
        functools.partial(_outproj_kernel, tiles_per_batch=tiles_per_batch),
        out_shape=jax.ShapeDtypeStruct((nt, D_MODEL), F32),
        grid=(nt // TM_OUT,),
        in_specs=[row(D_MODEL, 0),
                  pl.BlockSpec((SUBLANES, D_MODEL), lambda i: (0, 2)),
                  row(D_SSD, 0), row(D_SSD, 0), row(D_SSD, 0), row(D_SSD, COL_Z // D_SSD),
                  row(D_RET, 0), row(D_RET, 0), row(D_RET, COL_RET // 512 + 3),
                  row(D_GDN, 0), row(D_GDN, 0), row(D_GDN, COL_GG // 512),
                  vec(D_SSD), vec(D_SSD), vec(D_RET), vec(GDN_HEAD_DIM),
                  pl.BlockSpec((D_MODEL, D_MODEL), lambda i: (0, 0)),
                  vec(D_MODEL), vec(D_MODEL)],
        out_specs=row(D_MODEL, 0),
        scratch_shapes=[pltpu.VMEM((TM_OUT, D_MODEL), BF16)],
        compiler_params=_cparams(("arbitrary",)),
        name="outproj",
    )(x, mods_l, ys[0], ys[1], prep, p_main, yr[0], yr[1], p_main, og[0], og[1], p_main,
      dskip, snw, rnw, gnw, w_out, lng, lnb)


def _ffn_kernel(x_ref, sh_ref, sc_ref, gate_ref, w1_ref, w3_ref, w2_ref, lng_ref, lnb_ref, o_ref,
                h_scr, acc_scr, *, tiles_per_batch):
    f = pl.program_id(1)

    @pl.when(f == 0)
    def _():
        shift = _mod_row(sh_ref, tiles_per_batch)
        scale = _mod_row(sc_ref, tiles_per_batch)
        h_scr[...] = (_ln(x_ref[...]) * (1.0 + scale) + shift).astype(BF16)
        acc_scr[...] = jnp.zeros_like(acc_scr)

    h = h_scr[...]
    g = _silu(_dot(h, w1_ref[...])) * _dot(h, w3_ref[...])
    acc_scr[...] += _dot(g.astype(BF16), w2_ref[...])

    @pl.when(f == pl.num_programs(1) - 1)
    def _():
        gate = _mod_row(gate_ref, tiles_per_batch)
        o_ref[...] = _ln(DEEPNORM_ALPHA * x_ref[...] + gate * acc_scr[...]) * lng_ref[...] + lnb_ref[...]


def _ffn(x, mods_l, w1, w3, w2, lng, lnb, tiles_per_batch):
    nt = x.shape[0]
    mod = lambda k: pl.BlockSpec((SUBLANES, D_MODEL), lambda i, f: (0, k))
    vec = pl.BlockSpec((1, D_MODEL), lambda i, f: (0, 0))
    return pl.pallas_call(
        functools.partial(_ffn_kernel, tiles_per_batch=tiles_per_batch),
        out_shape=jax.ShapeDtypeStruct((nt, D_MODEL), F32),
        grid=(nt // TM, D_FF // TF),
        in_specs=[pl.BlockSpec((TM, D_MODEL), lambda i, f: (i, 0)), mod(3), mod(4), mod(5),
                  pl.BlockSpec((D_MODEL, TF), lambda i, f: (0, f)),
                  pl.BlockSpec((D_MODEL, TF), lambda i, f: (0, f)),
                  pl.BlockSpec((TF, D_MODEL), lambda i, f: (f, 0)),
                  vec, vec],
        out_specs=pl.BlockSpec((TM, D_MODEL), lambda i, f: (i, 0)),
        scratch_shapes=[pltpu.VMEM((TM, D_MODEL), BF16), pltpu.VMEM((TM, D_MODEL), F32)],
        compiler_params=_cparams(("arbitrary", "arbitrary")),
        name="ffn",
    )(x, mods_l, mods_l, mods_l, w1, w3, w2, lng, lnb)


def _top2(logits):
    lane = lax.broadcasted_iota(jnp.int32, logits.shape, 1).astype(F32)
    neg = jnp.where(lane < N_EXPERTS, logits, -jnp.inf)
    m1 = jnp.max(neg, axis=-1, keepdims=True)
    i1 = jnp.min(jnp.where(neg == m1, lane, float(LANES)), axis=-1, keepdims=True)
    rest = jnp.where(lane == i1, -jnp.inf, neg)
    m2 = jnp.max(rest, axis=-1, keepdims=True)
    i2 = jnp.min(jnp.where(rest == m2, lane, float(LANES)), axis=-1, keepdims=True)
    e2 = jnp.exp(m2 - m1)
    inv = 1.0 / (1.0 + e2)
    return (jnp.where(lane == 0.0, i1, 0.0) + jnp.where(lane == 1.0, i2, 0.0)
            + jnp.where(lane == 2.0, inv, 0.0) + jnp.where(lane == 3.0, e2 * inv, 0.0))


def _route_kernel(x_ref, sh_ref, sc_ref, r_ref, h_ref, info_ref, *, tiles_per_batch):
    shift = _mod_row(sh_ref, tiles_per_batch)
    scale = _mod_row(sc_ref, tiles_per_batch)
    h = _ln(x_ref[...]) * (1.0 + scale) + shift
    h_ref[...] = h
    info_ref[...] = _top2(_dot_hi(h, r_ref[...]))


def _route(x, mods_l, router, tiles_per_batch):
    nt = x.shape[0]
    mod = lambda k: pl.BlockSpec((SUBLANES, D_MODEL), lambda i: (0, k))
    return pl.pallas_call(
        functools.partial(_route_kernel, tiles_per_batch=tiles_per_batch),
        out_shape=(jax.ShapeDtypeStruct((nt, D_MODEL), F32), jax.ShapeDtypeStruct((nt, LANES), F32)),
        grid=(nt // TM,),
        in_specs=[pl.BlockSpec((TM, D_MODEL), lambda i: (i, 0)), mod(3), mod(4),
                  pl.BlockSpec((D_MODEL, LANES), lambda i: (0, 0))],
        out_specs=(pl.BlockSpec((TM, D_MODEL), lambda i: (i, 0)), pl.BlockSpec((TM, LANES), lambda i: (i, 0))),
        compiler_params=_cparams(("arbitrary",)),
        name="route",
    )(x, mods_l, mods_l, router)


def _dispatch(info):
    nt = info.shape[0]
    ef = info[:, :2].astype(jnp.int32).reshape(-1)
    onehot = (ef[:, None] == jnp.arange(N_EXPERTS, dtype=jnp.int32)[None, :]).astype(jnp.int32)
    rank = jnp.sum((jnp.cumsum(onehot, axis=0) - onehot) * onehot, axis=1)
    tiles_e = (jnp.sum(onehot, axis=0) + TMM - 1) // TMM
    tile_end = jnp.cumsum(tiles_e)
    n_used = tile_end[-1]
    dest = (tile_end - tiles_e)[ef] * TMM + rank
    n_tiles = (2 * nt) // TMM + N_EXPERTS
    tile_ids = jnp.minimum(jnp.arange(n_tiles, dtype=jnp.int32), n_used - 1)
    tile_expert = jnp.sum((tile_ids[:, None] >= tile_end[None, :]).astype(jnp.int32), axis=1)
    src = jnp.zeros((n_tiles * TMM,), jnp.int32).at[dest].set(jnp.arange(2 * nt, dtype=jnp.int32) // 2)
    return tile_expert, n_used.reshape(1), src.reshape(n_tiles, 1, TMM), dest.reshape(nt // TM, 1, 2 * TM)


def _row_gather(idx_ref, n_rows, src_hbm, dst_bufs, sem):
    nk = len(dst_bufs)

    def copy(r, k, row):
        return pltpu.make_async_copy(src_hbm.at[pl.ds(row, 1), :], dst_bufs[k].at[pl.ds(r, 1), :], sem)

    def start(blk, carry):
        for u in range(GATHER_UNROLL):
            r = blk * GATHER_UNROLL + u
            for k in range(nk):
                copy(r, k, idx_ref[0, 0, nk * r + k]).start(priority=(u * nk + k) % 2)
        return carry

    def wait(blk, carry):
        for u in range(GATHER_UNROLL):
            for k in range(nk):
                copy(blk * GATHER_UNROLL + u, k, 0).wait()
        return carry

    lax.fori_loop(0, n_rows // GATHER_UNROLL, start, 0)
    lax.fori_loop(0, n_rows // GATHER_UNROLL, wait, 0)


def _moe_ffn_kernel(te_ref, nu_ref, src_ref, h_hbm, w1_ref, w3_ref, w2_ref, o_ref,
                    xbuf, h_scr, acc_scr, sem):
    i = pl.program_id(0)
    f = pl.program_id(1)
    last_f = pl.num_programs(1) - 1
    used = i < nu_ref[0]

    @pl.when(jnp.logical_and(used, f == 0))
    def _():
        _row_gather(src_ref, TMM, h_hbm, (xbuf,), sem)
        h_scr[...] = xbuf[...].astype(BF16)
        acc_scr[...] = jnp.zeros_like(acc_scr)

    @pl.when(used)
    def _():
        h = h_scr[...]
        g = _silu(_dot(h, w1_ref[0, 0])) * _dot(h, w3_ref[0, 0])
        acc_scr[...] += _dot(g.astype(BF16), w2_ref[0, 0])

    @pl.when(f == last_f)
    def _():
        o_ref[...] = jnp.where(used, acc_scr[...], 0.0)


def _moe_ffn(h, tile_expert, n_used, src, w1, w3, w2, layer):
    n_tiles = src.shape[0]
    n_f = D_FF // TF
    fmap = lambda i, f, te, nu: jnp.where(i < nu[0], f, n_f - 1)
    return pl.pallas_call(
        _moe_ffn_kernel,
        out_shape=jax.ShapeDtypeStruct((n_tiles * TMM, D_MODEL), F32),
        grid_spec=pltpu.PrefetchScalarGridSpec(
            num_scalar_prefetch=2,
            grid=(n_tiles, n_f),
            in_specs=[pl.BlockSpec((1, 1, TMM), lambda i, f, te, nu: (i, 0, 0), memory_space=pltpu.SMEM),
                      pl.BlockSpec(memory_space=pl.ANY),
                      pl.BlockSpec((1, 1, D_MODEL, TF), lambda i, f, te, nu: (layer, te[i], 0, fmap(i, f, te, nu))),
                      pl.BlockSpec((1, 1, D_MODEL, TF), lambda i, f, te, nu: (layer, te[i], 0, fmap(i, f, te, nu))),
                      pl.BlockSpec((1, 1, TF, D_MODEL), lambda i, f, te, nu: (layer, te[i], fmap(i, f, te, nu), 0))],
            out_specs=pl.BlockSpec((TMM, D_MODEL), lambda i, f, te, nu: (i, 0)),
            scratch_shapes=[pltpu.VMEM((TMM, D_MODEL), F32), pltpu.VMEM((TMM, D_MODEL), BF16),
                            pltpu.VMEM((TMM, D_MODEL), F32), pltpu.SemaphoreType.DMA]),
        compiler_params=_cparams(("arbitrary", "arbitrary")),
        name="moe_ffn",
    )(tile_expert, n_used, src, h, w1, w3, w2)


def _moe_combine_kernel(dest_ref, x_ref, gate_ref, info_ref, ys_hbm, lng_ref, lnb_ref, o_ref,
                        buf0, buf1, sem, *, tiles_per_batch):
    _row_gather(dest_ref, TM, ys_hbm, (buf0, buf1), sem)
    info = info_ref[...]
    y = info[:, 2:3] * buf0[...] + info[:, 3:4] * buf1[...]
    gate = _mod_row(gate_ref, tiles_per_batch)
    o_ref[...] = _ln(DEEPNORM_ALPHA * x_ref[...] + gate * y) * lng_ref[...] + lnb_ref[...]


def _moe(x, mods_l, router, w1, w3, w2, layer, lng, lnb, tiles_per_batch):
    nt = x.shape[0]
    h, info = _route(x, mods_l, router, tiles_per_batch)
    tile_expert, n_used, src, dest = _dispatch(info)
    ys = _moe_ffn(h, tile_expert, n_used, src, w1, w3, w2, layer)
    vec = pl.BlockSpec((1, D_MODEL), lambda i: (0, 0))
    return pl.pallas_call(
        functools.partial(_moe_combine_kernel, tiles_per_batch=tiles_per_batch),
        out_shape=jax.ShapeDtypeStruct((nt, D_MODEL), F32),
        grid=(nt // TM,),
        in_specs=[pl.BlockSpec((1, 1, 2 * TM), lambda i: (i, 0, 0), memory_space=pltpu.SMEM),
                  pl.BlockSpec((TM, D_MODEL), lambda i: (i, 0)),
                  pl.BlockSpec((SUBLANES, D_MODEL), lambda i: (0, 5)),
                  pl.BlockSpec((TM, LANES), lambda i: (i, 0)),
                  pl.BlockSpec(memory_space=pl.ANY),
                  vec, vec],
        out_specs=pl.BlockSpec((TM, D_MODEL), lambda i: (i, 0)),
        scratch_shapes=[pltpu.VMEM((TM, D_MODEL), F32), pltpu.VMEM((TM, D_MODEL), F32),
                        pltpu.SemaphoreType.DMA],
        compiler_params=_cparams(("arbitrary",)),
        name="moe_combine",
    )(dest, x, mods_l, info, ys, lng, lnb)


def _rope_tables(l):
    n_freq = RET_HEAD_DIM // 4
    inv_freq = ROPE_BASE ** (-jnp.arange(n_freq, dtype=F32) / n_freq)
    pos = jnp.arange(l, dtype=jnp.int32)
    row = (pos // GRID_W).astype(F32)
    col = (pos % GRID_W).astype(F32)
    ang = jnp.concatenate([row[:, None] * inv_freq, col[:, None] * inv_freq], axis=-1)
    cos, sin = jnp.cos(ang), jnp.sin(ang)
    cos_t = jnp.concatenate([cos, cos], axis=-1)
    sin_t = jnp.concatenate([-sin, sin], axis=-1)
    cos_t = jnp.concatenate([cos_t, jnp.ones((PREP_ROWS, LANES), F32)], axis=0)
    sin_t = jnp.concatenate([sin_t, jnp.zeros((PREP_ROWS, LANES), F32)], axis=0)
    return cos_t, sin_t


def _split_w_in(w):
    o_ret = D_SSD + SSD_XBC + 2 * SSD_HEADS
    o_gdn = o_ret + 4 * D_RET
    z = w[:, 0:D_SSD]
    xbc = w[:, D_SSD:D_SSD + SSD_XBC]
    dt = w[:, D_SSD + SSD_XBC:o_ret]
    ret = w[:, o_ret:o_gdn]
    gqkv = w[:, o_gdn:o_gdn + 3 * D_GDN]
    gg = w[:, o_gdn + 3 * D_GDN:o_gdn + 4 * D_GDN]
    ab = w[:, o_gdn + 4 * D_GDN:]
    main = jnp.concatenate([xbc, gqkv, z, ret, gg], axis=1).astype(BF16)
    small = jnp.concatenate([dt, ab, jnp.zeros((w.shape[0], LANES - dt.shape[1] - ab.shape[1]), w.dtype)], axis=1)
    return main, small.astype(BF16), small.T.astype(BF16)


def _lane_params(ssd_vals, gdn_vals):
    row = jnp.zeros((LANES,), F32)
    row = row.at[SM_DT:SM_DT + 2 * SSD_HEADS].set(ssd_vals.reshape(-1))
    row = row.at[SM_A:SM_A + 2 * GDN_HEADS].set(gdn_vals.reshape(-1))
    return row


def kernel(x, c, ctx, c_ctx, w_mod, b_mod, w_in, ssd_conv_w, ssd_conv_b, ssd_a_log, ssd_dt_bias, ssd_d, ssd_norm_w, ret_log_decay, ret_norm_w, gdn_conv_w, gdn_a_log, gdn_dt_bias, gdn_norm_w, w_out, ln1_g, ln1_b, ln2_g, ln2_b, ffn_w1, ffn_w3, ffn_w2, moe_router, moe_w1, moe_w3, moe_w2):
    b, l, d = x.shape
    lc = ctx.shape[1]
    assert d == D_MODEL and l % TM == 0 and (b * lc) % TM == 0 and lc % PREP_ROWS == 0 and l % GRID_W == 0
    tiles_per_batch = l // TM

    xt = jnp.concatenate([x.reshape(b * l, d), ctx.reshape(b * lc, d)], axis=0)
    mods = _mods(c, c_ctx, w_mod, b_mod)
    cos_t, sin_t = _rope_tables(l)
    moe_w1_16, moe_w3_16, moe_w2_16 = (w.astype(BF16) for w in (moe_w1, moe_w3, moe_w2))

    for i in range(DEPTH):
        w_main, w_small, w_small_t = _split_w_in(w_in[i])
        conv_w = jnp.concatenate([ssd_conv_w[i], gdn_conv_w[i]], axis=1)
        conv_b = jnp.concatenate([ssd_conv_b[i], jnp.zeros((3 * D_GDN,), F32)])[None, :]
        bias_row = _lane_params(ssd_dt_bias[i], gdn_dt_bias[i])
        alog_row = _lane_params(ssd_a_log[i], gdn_a_log[i])
        prow = jnp.zeros((SUBLANES, LANES), F32).at[0].set(bias_row).at[1].set(alog_row)
        pcol = prow.T
        ld = jnp.broadcast_to(ret_log_decay[i].reshape(2 * RET_HEADS, 1), (2 * RET_HEADS, LANES))

        p_main, p_small, p_small_t = _inproj(xt, mods[i], w_main, w_small, w_small_t, tiles_per_batch)
        prep = _prep(p_main, cos_t, sin_t, conv_w, conv_b, b, l, lc)
        ys = _ssd_scan(prep, p_small, p_small_t, prow, pcol, b, l, lc)
        yr = _ret_scan(prep, p_main, ld, b, l, lc)
        og = _gdn_scan(prep, p_small, p_small_t, prow, pcol, b, l, lc)
        xt = _outproj(xt, mods[i], ys, prep, p_main, yr, og,
                      jnp.repeat(ssd_d[i], SSD_HEAD_DIM)[None, :], ssd_norm_w[i][None, :],
                      ret_norm_w[i][None, :], gdn_norm_w[i][None, :], w_out[i].astype(BF16),
                      ln1_g[i][None, :], ln1_b[i][None, :], l // TM_OUT)
        j = i // 2
        if i % 2 == 0:
            xt = _ffn(xt, mods[i], ffn_w1[j].astype(BF16), ffn_w3[j].astype(BF16), ffn_w2[j].astype(BF16),
                      ln2_g[i][None, :], ln2_b[i][None, :], tiles_per_batch)
        else:
            router = jnp.concatenate([moe_router[j], jnp.zeros((d, LANES - N_EXPERTS), F32)], axis=1)
            xt = _moe(xt, mods[i], router, moe_w1_16, moe_w3_16, moe_w2_16, j,
                      ln2_g[i][None, :], ln2_b[i][None, :], tiles_per_batch)
    return xt[:b * l].reshape(b, l, d)
```
